```python
import jax, jax.numpy as jnp
from jax import lax
import numpy as np

D_MODEL = 2048
BATCH = 8
SEQ = 2048
DEPTH = 4

FFN_HIDDEN = 5632
GROUP_DIM = 128
A_WIDTH = D_MODEL // 2
A_GROUPS = A_WIDTH // GROUP_DIM
A_KERNEL = 31
B_WIDTH = D_MODEL // 2
B_GROUPS = B_WIDTH // GROUP_DIM
B_KERNEL = 3
MIX_WIDTH_EVEN = A_WIDTH + B_WIDTH
IN_EVEN = 2 * A_WIDTH + 3 * B_WIDTH
C_WIDTH = 2 * D_MODEL
C_HEADS = 16
C_HEAD_DIM = C_WIDTH // C_HEADS
CHUNK = 128
N_SUB = 3
N_EVEN = (DEPTH + 1) // 2
N_ODD = DEPTH // 2
EPS = 1e-6
ADA_SCALE = 0.1

kernel_name = "hybrid_conv_gmlp_macaron_adaln"


def rms_norm(x, g):
    xf = x.astype(jnp.float32)
    y = xf * lax.rsqrt(jnp.mean(xf * xf, axis=-1, keepdims=True) + EPS)
    return (y * g.astype(jnp.float32)).astype(x.dtype)


def layer_norm(x, g, b):
    xf = x.astype(jnp.float32)
    mu = jnp.mean(xf, axis=-1, keepdims=True)
    xc = xf - mu
    var = jnp.mean(xc * xc, axis=-1, keepdims=True)
    y = xc * lax.rsqrt(var + EPS) * g.astype(jnp.float32) + b.astype(jnp.float32)
    return y.astype(x.dtype)


def modulate(h, shift, scale):
    return h * (1 + scale[:, None, :]) + shift[:, None, :]


def causal_depthwise_conv(x, w):
    k, ch = w.shape
    return lax.conv_general_dilated(
        x, w[:, None, :].astype(x.dtype), window_strides=(1,),
        padding=((k - 1, 0),), dimension_numbers=("NWC", "WIO", "NWC"),
        feature_group_count=ch)


def swiglu(h, w_gate, w_up, w_down):
    return (jax.nn.silu(h @ w_gate) * (h @ w_up)) @ w_down


def even_mixer(h, w_in, a_conv, a_ln_g, a_ln_b, b_conv, w_out):
    z = h @ w_in
    a_val, a_gate, b_b, b_c, b_x = jnp.split(
        z, [A_WIDTH, 2 * A_WIDTH, 2 * A_WIDTH + B_WIDTH, 2 * A_WIDTH + 2 * B_WIDTH], axis=-1)
    a = a_val * jax.nn.sigmoid(a_gate)
    a = causal_depthwise_conv(a, a_conv)
    a = jax.nn.silu(layer_norm(a, a_ln_g, a_ln_b))
    bo = b_b * causal_depthwise_conv(b_c * b_x, b_conv)
    return jnp.concatenate([a, bo], axis=-1) @ w_out


def odd_mixer(h, w_in, v_ln_g, v_ln_b, w_s, b_s, w_out):
    bsz, s, _ = h.shape
    z = jax.nn.gelu(h @ w_in, approximate=False)
    u, v = jnp.split(z, 2, axis=-1)
    v = layer_norm(v, v_ln_g, v_ln_b)
    v = v.reshape(bsz, s // CHUNK, CHUNK, C_HEADS, C_HEAD_DIM)
    causal = jnp.tril(jnp.ones((CHUNK, CHUNK), dtype=bool))
    ws = jnp.where(causal[None], w_s, 0).astype(v.dtype)
    vs = jnp.einsum("hts,bnshd->bnthd", ws, v) + b_s.T[:, :, None].astype(v.dtype)
    vs = vs.reshape(bsz, s, C_WIDTH)
    return (u * vs) @ w_out


def setup_inputs(seed: int = 0) -> dict:
    key = jax.random.key(seed)
    ks = jax.random.split(key, 24)
    f32 = jnp.float32
    nrm = lambda k, shape, scale: jax.random.normal(k, shape, f32) * scale
    d, f = D_MODEL, FFN_HIDDEN
    return {
        "x": nrm(ks[0], (BATCH, SEQ, d), 1.0),
        "c": nrm(ks[1], (BATCH, d), 1.0),
        "ada_w": nrm(ks[2], (DEPTH, d, N_SUB * 3 * d), ADA_SCALE * d ** -0.5),
        "ada_b": nrm(ks[3], (DEPTH, N_SUB * 3 * d), 0.01),
        "norm_g": 1.0 + nrm(ks[4], (DEPTH, N_SUB, d), 0.02),
        "ffn_w_gate": nrm(ks[5], (DEPTH, 2, d, f), d ** -0.5),
        "ffn_w_up": nrm(ks[6], (DEPTH, 2, d, f), d ** -0.5),
        "ffn_w_down": nrm(ks[7], (DEPTH, 2, f, d), f ** -0.5),
        "ev_w_in": nrm(ks[8], (N_EVEN, d, IN_EVEN), d ** -0.5),
        "ev_a_conv": nrm(ks[9], (N_EVEN, A_KERNEL, A_WIDTH), A_KERNEL ** -0.5),
        "ev_a_ln_g": 1.0 + nrm(ks[10], (N_EVEN, A_WIDTH), 0.02),
        "ev_a_ln_b": nrm(ks[11], (N_EVEN, A_WIDTH), 0.02),
        "ev_b_conv": nrm(ks[12], (N_EVEN, B_KERNEL, B_WIDTH), B_KERNEL ** -0.5),
        "ev_w_out": nrm(ks[13], (N_EVEN, MIX_WIDTH_EVEN, d), MIX_WIDTH_EVEN ** -0.5),
        "od_w_in": nrm(ks[14], (N_ODD, d, 2 * C_WIDTH), d ** -0.5),
        "od_v_ln_g": 1.0 + nrm(ks[15], (N_ODD, C_WIDTH), 0.02),
        "od_v_ln_b": nrm(ks[16], (N_ODD, C_WIDTH), 0.02),
        "od_w_s": nrm(ks[17], (N_ODD, C_HEADS, CHUNK, CHUNK), 0.5 * CHUNK ** -0.5),
        "od_b_s": 1.0 + nrm(ks[18], (N_ODD, C_HEADS, CHUNK), 0.02),
        "od_w_out": nrm(ks[19], (N_ODD, C_WIDTH, d), C_WIDTH ** -0.5),
        "final_g": 1.0 + nrm(ks[20], (d,), 0.02),
    }


def reference(x, c, ada_w, ada_b, norm_g, ffn_w_gate, ffn_w_up, ffn_w_down,
              ev_w_in, ev_a_conv, ev_a_ln_g, ev_a_ln_b, ev_b_conv, ev_w_out,
              od_w_in, od_v_ln_g, od_v_ln_b, od_w_s, od_b_s, od_w_out, final_g):
    bsz = x.shape[0]
    cond = jax.nn.silu(c)
    for l in range(DEPTH):
        mod = (cond @ ada_w[l] + ada_b[l]).reshape(bsz, N_SUB, 3, D_MODEL)
        h = modulate(rms_norm(x, norm_g[l, 0]), mod[:, 0, 0], mod[:, 0, 1])
        x = x + 0.5 * (1 + mod[:, 0, 2])[:, None, :] * swiglu(
            h, ffn_w_gate[l, 0], ffn_w_up[l, 0], ffn_w_down[l, 0])
        h = modulate(rms_norm(x, norm_g[l, 1]), mod[:, 1, 0], mod[:, 1, 1])
        j = l // 2
        if l % 2 == 0:
            y = even_mixer(h, ev_w_in[j], ev_a_conv[j], ev_a_ln_g[j], ev_a_ln_b[j],
                           ev_b_conv[j], ev_w_out[j])
        else:
            y = odd_mixer(h, od_w_in[j], od_v_ln_g[j], od_v_ln_b[j], od_w_s[j],
                          od_b_s[j], od_w_out[j])
        x = x + (1 + mod[:, 1, 2])[:, None, :] * y
        h = modulate(rms_norm(x, norm_g[l, 2]), mod[:, 2, 0], mod[:, 2, 1])
        x = x + 0.5 * (1 + mod[:, 2, 2])[:, None, :] * swiglu(
            h, ffn_w_gate[l, 1], ffn_w_up[l, 1], ffn_w_down[l, 1])
    return rms_norm(x, final_g)
```

```python
import functools

import jax
import jax.numpy as jnp
from jax import lax
from jax.experimental import pallas as pl
from jax.experimental.pallas import tpu as pltpu

D_MODEL = 2048
SEQ = 2048
FFN_HIDDEN = 5632
A_WIDTH = 1024
B_WIDTH = 1024
A_KERNEL = 31
B_KERNEL = 3
IN_EVEN = 2 * A_WIDTH + 3 * B_WIDTH
C_WIDTH = 4096
C_HEADS = 16
C_HEAD_DIM = C_WIDTH // C_HEADS
CHUNK = 128
N_SUB = 3
EPS = 1e-6

F32 = jnp.float32
BF16 = jnp.bfloat16

SUBLANES = 8
A_HALO = 32
B_HALO = 8
VMEM_LIMIT = 56 * 1024 * 1024


def _cparams(*sem):
    return pltpu.CompilerParams(dimension_semantics=sem, vmem_limit_bytes=VMEM_LIMIT)


def _silu(v):
    return v * jax.nn.sigmoid(v)


def _norm_modulate(x, g, shift, scale):
    ms = jnp.mean(x * x, axis=-1, keepdims=True)
    y = x * lax.rsqrt(ms + EPS) * g
    return y * (1.0 + scale) + shift


def _mod_kernel(c_ref, w_ref, b_ref, o_ref):
    cond = _silu(c_ref[...]).astype(BF16)
    o_ref[...] = jnp.dot(cond, w_ref[...].astype(BF16), preferred_element_type=F32) + b_ref[...]


def _modulation(c, ada_w, ada_b, tn=1024):
    depth, d, n = ada_w.shape
    bsz = c.shape[0]
    return pl.pallas_call(
        _mod_kernel,
        grid=(depth, n // tn),
        in_specs=[
            pl.BlockSpec((bsz, d), lambda l, j: (0, 0)),
            pl.BlockSpec((None, d, tn), lambda l, j: (l, 0, j)),
            pl.BlockSpec((None, 1, tn), lambda l, j: (l, 0, j)),
        ],
        out_specs=pl.BlockSpec((None, bsz, tn), lambda l, j: (l, 0, j)),
        out_shape=jax.ShapeDtypeStruct((depth, bsz, n), F32),
        compiler_params=_cparams("arbitrary", "arbitrary"),
        name="modulation",
    )(c, ada_w, ada_b.reshape(depth, 1, n))


def _mod_spec(l, sub, kind, rows_per_batch_tile):
    k = sub * 3 + kind
    return pl.BlockSpec((None, None, None, 1, D_MODEL),
                        lambda i, *_: (l, i // rows_per_batch_tile, k, 0, 0))


def _ffn_kernel(x_ref, shift_ref, scale_ref, gate_ref, g_ref, wg_ref, wu_ref, wd_ref,
                fg_ref, o_ref, h_ref, acc_ref, *, final):
    j = pl.program_id(1)

    @pl.when(j == 0)
    def _():
        h = _norm_modulate(x_ref[...], g_ref[...], shift_ref[...], scale_ref[...])
        h_ref[...] = h.astype(BF16)
        acc_ref[...] = jnp.zeros_like(acc_ref)

    h = h_ref[...]
    gate = jnp.dot(h, wg_ref[...], preferred_element_type=F32)
    up = jnp.dot(h, wu_ref[...], preferred_element_type=F32)
    a = (_silu(gate) * up).astype(BF16)
    acc_ref[...] += jnp.dot(a, wd_ref[...], preferred_element_type=F32)

    @pl.when(j == pl.num_programs(1) - 1)
    def _():
        y = x_ref[...] + (0.5 * (1.0 + gate_ref[...])) * acc_ref[...]
        if final:
            ms = jnp.mean(y * y, axis=-1, keepdims=True)
            y = y * lax.rsqrt(ms + EPS) * fg_ref[...]
        o_ref[...] = y


def _ffn(x2, mod5, norm_g4, wg, wu, wd, final_g, l, s, sub, final, tm=512, tf=512):
    rows, d = x2.shape
    f = wg.shape[-1]
    tps = SEQ // tm
    return pl.pallas_call(
        functools.partial(_ffn_kernel, final=final),
        grid=(rows // tm, f // tf),
        in_specs=[
            pl.BlockSpec((tm, d), lambda i, j: (i, 0)),
            _mod_spec(l, sub, 0, tps), _mod_spec(l, sub, 1, tps), _mod_spec(l, sub, 2, tps),
            pl.BlockSpec((None, None, 1, d), lambda i, j: (l, sub, 0, 0)),
            pl.BlockSpec((None, None, d, tf), lambda i, j: (l, s, 0, j)),
            pl.BlockSpec((None, None, d, tf), lambda i, j: (l, s, 0, j)),
            pl.BlockSpec((None, None, tf, d), lambda i, j: (l, s, j, 0)),
            pl.BlockSpec((1, d), lambda i, j: (0, 0)),
        ],
        out_specs=pl.BlockSpec((tm, d), lambda i, j: (i, 0)),
        out_shape=jax.ShapeDtypeStruct((rows, d), F32),
        scratch_shapes=[pltpu.VMEM((tm, d), BF16), pltpu.VMEM((tm, d), F32)],
        compiler_params=_cparams("arbitrary", "arbitrary"),
        name="ffn",
    )(x2, mod5, mod5, mod5, norm_g4, wg, wu, wd, final_g)


def _causal_conv(buf_ref, w_ref, out_ref, *, taps, halo, rows, width, rblk=64, cblk=128):
    base = halo - (taps - 1)
    win = rblk + halo
    for c0 in range(0, width, cblk):
        wts = w_ref[:, c0:c0 + cblk]

        def body(rb, carry, c0=c0, wts=wts):
            r0 = pl.multiple_of(rb * rblk, rblk)
            window = buf_ref[pl.ds(r0, win), c0:c0 + cblk]
            acc = jnp.zeros((rblk, cblk), F32)
            for r in range(SUBLANES):
                ks = [k for k in range(taps) if (base + k) % SUBLANES == r]
                if not ks:
                    continue
                shifted = window if r == 0 else pltpu.roll(window, win - r, axis=0)
                for k in ks:
                    q = (base + k) // SUBLANES * SUBLANES
                    acc = acc + wts[k:k + 1, :] * shifted[q:q + rblk, :]
            out_ref[pl.ds(r0, rblk), c0:c0 + cblk] = acc
            return carry

        lax.fori_loop(0, rows // rblk, body, 0)


def _even_kernel(x_ref, shift_ref, scale_ref, g_ref, win_ref, aconv_ref, lng_ref, lnb_ref,
                 bconv_ref, o_ref, abuf, cbuf, bb_ref, conv_ref, *, tm, tiles_per_seq):
    i = pl.program_id(0)
    first = (i % tiles_per_seq) == 0

    @pl.when(first)
    def _():
        abuf[0:A_HALO, :] = jnp.zeros((A_HALO, A_WIDTH), F32)
        cbuf[0:B_HALO, :] = jnp.zeros((B_HALO, B_WIDTH), F32)

    @pl.when(jnp.logical_not(first))
    def _():
        abuf[0:A_HALO, :] = abuf[tm:tm + A_HALO, :]
        cbuf[0:B_HALO, :] = cbuf[tm:tm + B_HALO, :]

    h = _norm_modulate(x_ref[...], g_ref[...], shift_ref[...], scale_ref[...]).astype(BF16)

    def proj(k):
        return jnp.dot(h, win_ref[:, k * A_WIDTH:(k + 1) * A_WIDTH], preferred_element_type=F32)

    abuf[A_HALO:A_HALO + tm, :] = proj(0) * jax.nn.sigmoid(proj(1))
    bb_ref[...] = proj(2)
    cbuf[B_HALO:B_HALO + tm, :] = proj(3) * proj(4)

    _causal_conv(abuf, aconv_ref, conv_ref, taps=A_KERNEL, halo=A_HALO, rows=tm, width=A_WIDTH)
    a = conv_ref[...]
    mu = jnp.mean(a, axis=-1, keepdims=True)
    ac = a - mu
    var = jnp.mean(ac * ac, axis=-1, keepdims=True)
    a = ac * lax.rsqrt(var + EPS) * lng_ref[...] + lnb_ref[...]
    o_ref[:, 0:A_WIDTH] = _silu(a).astype(BF16)

    _causal_conv(cbuf, bconv_ref, conv_ref, taps=B_KERNEL, halo=B_HALO, rows=tm, width=B_WIDTH)
    o_ref[:, A_WIDTH:A_WIDTH + B_WIDTH] = (bb_ref[...] * conv_ref[...]).astype(BF16)


def _even_mix(x2, mod5, norm_g4, w_in, a_conv, ln_g, ln_b, b_conv, l, jx, tm=256):
    rows, d = x2.shape
    tps = SEQ // tm
    return pl.pallas_call(
        functools.partial(_even_kernel, tm=tm, tiles_per_seq=tps),
        grid=(rows // tm,),
        in_specs=[
            pl.BlockSpec((tm, d), lambda i: (i, 0)),
            _mod_spec(l, 1, 0, tps), _mod_spec(l, 1, 1, tps),
            pl.BlockSpec((None, None, 1, d), lambda i: (l, 1, 0, 0)),
            pl.BlockSpec((None, d, IN_EVEN), lambda i: (jx, 0, 0), pipeline_mode=pl.Buffered(1)),
            pl.BlockSpec((None, A_KERNEL, A_WIDTH), lambda i: (jx, 0, 0)),
            pl.BlockSpec((None, 1, A_WIDTH), lambda i: (jx, 0, 0)),
            pl.BlockSpec((None, 1, A_WIDTH), lambda i: (jx, 0, 0)),
            pl.BlockSpec((None, B_KERNEL, B_WIDTH), lambda i: (jx, 0, 0)),
        ],
        out_specs=pl.BlockSpec((tm, A_WIDTH + B_WIDTH), lambda i: (i, 0)),
        out_shape=jax.ShapeDtypeStruct((rows, A_WIDTH + B_WIDTH), BF16),
        scratch_shapes=[
            pltpu.VMEM((A_HALO + tm, A_WIDTH), F32),
            pltpu.VMEM((B_HALO + tm, B_WIDTH), F32),
            pltpu.VMEM((tm, B_WIDTH), F32),
            pltpu.VMEM((tm, A_WIDTH), F32),
        ],
        compiler_params=_cparams("arbitrary"),
        name="even_mix",
    )(x2, mod5, mod5, norm_g4, w_in, a_conv, ln_g, ln_b, b_conv)


def _odd_kernel(x_ref, shift_ref, scale_ref, g_ref, win_ref, lng_ref, lnb_ref, ws_ref, bs_ref,
                o_ref, h_ref, z_ref, *, tm, tn):
    j = pl.program_id(1)

    @pl.when(j == 0)
    def _():
        h = _norm_modulate(x_ref[...], g_ref[...], shift_ref[...], scale_ref[...])
        h_ref[...] = h.astype(BF16)

    z = jnp.dot(h_ref[...], win_ref[...], preferred_element_type=F32)
    z_ref[j] = 0.5 * z * (1.0 + lax.erf(z * (0.5 ** 0.5)))

    @pl.when(j == pl.num_programs(1) - 1)
    def _():
        nt = C_WIDTH // tn
        heads_per_tile = tn // C_HEAD_DIM
        row = lax.broadcasted_iota(jnp.int32, (CHUNK, CHUNK), 0)
        col = lax.broadcasted_iota(jnp.int32, (CHUNK, CHUNK), 1)
        causal = col <= row
        for n in range(tm // CHUNK):
            rs = slice(n * CHUNK, (n + 1) * CHUNK)
            total = jnp.zeros((CHUNK, 1), F32)
            for t in range(nt):
                total = total + jnp.sum(z_ref[nt + t, rs, :], axis=-1, keepdims=True)
            mu = total * (1.0 / C_WIDTH)
            sq = jnp.zeros((CHUNK, 1), F32)
            for t in range(nt):
                vc = z_ref[nt + t, rs, :] - mu
                sq = sq + jnp.sum(vc * vc, axis=-1, keepdims=True)
            rstd = lax.rsqrt(sq * (1.0 / C_WIDTH) + EPS)
            for t in range(nt):
                vn = ((z_ref[nt + t, rs, :] - mu) * rstd * lng_ref[:, t * tn:(t + 1) * tn]
                      + lnb_ref[:, t * tn:(t + 1) * tn]).astype(BF16)
                for hh in range(heads_per_tile):
                    hd = t * heads_per_tile + hh
                    cs = slice(hh * C_HEAD_DIM, (hh + 1) * C_HEAD_DIM)
                    ws = jnp.where(causal, ws_ref[hd], 0.0).astype(BF16)
                    vs = jnp.dot(ws, vn[:, cs], preferred_element_type=F32) + bs_ref[hd]
                    u = z_ref[t, rs, cs]
                    o_ref[rs, t * tn + hh * C_HEAD_DIM:t * tn + (hh + 1) * C_HEAD_DIM] = (
                        u * vs).astype(BF16)


def _odd_mix(x2, mod5, norm_g4, w_in, ln_g, ln_b, w_s, b_s3, l, jx, tm=512, tn=1024):
    rows, d = x2.shape
    tps = SEQ // tm
    nj = 2 * C_WIDTH // tn
    return pl.pallas_call(
        functools.partial(_odd_kernel, tm=tm, tn=tn),
        grid=(rows // tm, nj),
        in_specs=[
            pl.BlockSpec((tm, d), lambda i, j: (i, 0)),
            _mod_spec(l, 1, 0, tps), _mod_spec(l, 1, 1, tps),
            pl.BlockSpec((None, None, 1, d), lambda i, j: (l, 1, 0, 0)),
            pl.BlockSpec((None, d, tn), lambda i, j: (jx, 0, j)),
            pl.BlockSpec((None, 1, C_WIDTH), lambda i, j: (jx, 0, 0)),
            pl.BlockSpec((None, 1, C_WIDTH), lambda i, j: (jx, 0, 0)),
            pl.BlockSpec((None, C_HEADS, CHUNK, CHUNK), lambda i, j: (jx, 0, 0, 0)),
            pl.BlockSpec((None, C_HEADS, CHUNK, 1), lambda i, j: (jx, 0, 0, 0)),
        ],
        out_specs=pl.BlockSpec((tm, C_WIDTH), lambda i, j: (i, 0)),
        out_shape=jax.ShapeDtypeStruct((rows, C_WIDTH), BF16),
        scratch_shapes=[pltpu.VMEM((tm, d), BF16), pltpu.VMEM((nj, tm, tn), F32)],
        compiler_params=_cparams("arbitrary", "arbitrary"),
        name="odd_mix",
    )(x2, mod5, mod5, norm_g4, w_in, ln_g, ln_b, w_s, b_s3)


def _proj_kernel(x_ref, m_ref, gate_ref, w_ref, o_ref):
    y = jnp.dot(m_ref[...], w_ref[...], preferred_element_type=F32)
    o_ref[...] = x_ref[...] + (1.0 + gate_ref[...]) * y


def _proj_residual(x2, m, mod5, w_out, l, jx, tm=512, tn=1024):
    rows, d = x2.shape
    k = m.shape[-1]
    tps = SEQ // tm
    kk = 1 * 3 + 2
    return pl.pallas_call(
        _proj_kernel,
        grid=(d // tn, rows // tm),
        in_specs=[
            pl.BlockSpec((tm, tn), lambda n, i: (i, n)),
            pl.BlockSpec((tm, k), lambda n, i: (i, 0)),
            pl.BlockSpec((None, None, None, 1, tn), lambda n, i: (l, i // tps, kk, 0, n)),
            pl.BlockSpec((None, k, tn), lambda n, i: (jx, 0, n)),
        ],
        out_specs=pl.BlockSpec((tm, tn), lambda n, i: (i, n)),
        out_shape=jax.ShapeDtypeStruct((rows, d), F32),
        compiler_params=_cparams("arbitrary", "arbitrary"),
        name="proj_residual",
    )(x2, m, mod5, w_out)


def kernel(x, c, ada_w, ada_b, norm_g, ffn_w_gate, ffn_w_up, ffn_w_down, ev_w_in, ev_a_conv,
           ev_a_ln_g, ev_a_ln_b, ev_b_conv, ev_w_out, od_w_in, od_v_ln_g, od_v_ln_b, od_w_s,
           od_b_s, od_w_out, final_g):
    bsz, seq, d = x.shape
    depth = ada_w.shape[0]
    assert (seq, d) == (SEQ, D_MODEL)

    mod5 = _modulation(c, ada_w, ada_b).reshape(depth, bsz, N_SUB * 3, 1, d)
    norm_g4 = norm_g.reshape(depth, N_SUB, 1, d)
    final_g2 = final_g.reshape(1, d)
    wg, wu, wd = (w.astype(BF16) for w in (ffn_w_gate, ffn_w_up, ffn_w_down))
    ev_in, ev_out = ev_w_in.astype(BF16), ev_w_out.astype(BF16)
    od_in, od_out = od_w_in.astype(BF16), od_w_out.astype(BF16)
    ev_ln_g, ev_ln_b = (v.reshape(-1, 1, A_WIDTH) for v in (ev_a_ln_g, ev_a_ln_b))
    od_ln_g, od_ln_b = (v.reshape(-1, 1, C_WIDTH) for v in (od_v_ln_g, od_v_ln_b))
    od_bs = od_b_s.reshape(-1, C_HEADS, CHUNK, 1)

    x2 = x.reshape(bsz * seq, d)
    for l in range(depth):
        x2 = _ffn(x2, mod5, norm_g4, wg, wu, wd, final_g2, l, 0, 0, False)
        jx = l // 2
        if l % 2 == 0:
            m = _even_mix(x2, mod5, norm_g4, ev_in, ev_a_conv, ev_ln_g, ev_ln_b, ev_b_conv, l, jx)
            x2 = _proj_residual(x2, m, mod5, ev_out, l, jx)
        else:
            m = _odd_mix(x2, mod5, norm_g4, od_in, od_ln_g, od_ln_b, od_w_s, od_bs, l, jx)
            x2 = _proj_residual(x2, m, mod5, od_out, l, jx)
        x2 = _ffn(x2, mod5, norm_g4, wg, wu, wd, final_g2, l, 1, 2, l == depth - 1)
    return x2.reshape(bsz, seq, d)
```

```python
import functools

import jax
import jax.numpy as jnp
from jax import lax
from jax.experimental import pallas as pl
from jax.experimental.pallas import tpu as pltpu

D_MODEL = 2048
SEQ = 2048
FFN_HIDDEN = 5632
A_WIDTH = 1024
B_WIDTH = 1024
A_KERNEL = 31
B_KERNEL = 3
C_WIDTH = 4096
C_HEAD_DIM = 256
CHUNK = 128
N_SUB = 3
EPS = 1e-6

F32 = jnp.float32
BF16 = jnp.bfloat16

SUBLANES = 8
LANES = 128
A_HALO = 32
B_HALO = 8
CONV_ROWS = 64
VMEM_LIMIT = 56 * 1024 * 1024

FFN_TM = 512
FFN_TF = 512
EDGE_SPLIT = 2
EVEN_TM = 256
ODD_TM = 512
ODD_TN = 1024
ODD_SPLIT = 2
MOD_TN = 1024


def _cparams(*sem):
    return pltpu.CompilerParams(dimension_semantics=sem, vmem_limit_bytes=VMEM_LIMIT)


def _silu(v):
    return v * jax.nn.sigmoid(v)


def _gelu(v):
    return 0.5 * v * (1.0 + lax.erf(v * (0.5 ** 0.5)))


def _norm_modulate(x, g, shift, scale):
    ms = jnp.mean(x * x, axis=-1, keepdims=True)
    y = x * lax.rsqrt(ms + EPS) * g
    return y * (1.0 + scale) + shift


def _row_slices(rows, parts):
    sub = rows // parts
    return [slice(r * sub, (r + 1) * sub) for r in range(parts)]


def _mod_kernel(c_ref, w_ref, b_ref, o_ref):
    cond = _silu(c_ref[...]).astype(BF16)
    o_ref[...] = jnp.dot(cond, w_ref[...].astype(BF16), preferred_element_type=F32) + b_ref[...]


def _modulation(c, ada_w, ada_b):
    depth, d, n = ada_w.shape
    bsz = c.shape[0]
    tn = MOD_TN
    return pl.pallas_call(
        _mod_kernel,
        grid=(depth, n // tn),
        in_specs=[
            pl.BlockSpec((bsz, d), lambda l, j: (0, 0)),
            pl.BlockSpec((None, d, tn), lambda l, j: (l, 0, j)),
            pl.BlockSpec((None, 1, tn), lambda l, j: (l, 0, j)),
        ],
        out_specs=pl.BlockSpec((None, bsz, tn), lambda l, j: (l, 0, j)),
        out_shape=jax.ShapeDtypeStruct((depth, bsz, n), F32),
        compiler_params=_cparams("arbitrary", "arbitrary"),
        name="modulation",
    )(c, ada_w, ada_b.reshape(depth, 1, n))


def _mod_spec(l, sub, kind, rows_per_batch_tile):
    k = sub * 3 + kind
    return pl.BlockSpec((None, None, None, 1, D_MODEL),
                        lambda i, *_: (l, i // rows_per_batch_tile, k, 0, 0))


def _ffn_kernel(x_ref, shift_ref, scale_ref, gate_ref, g_ref, wg_ref, wu_ref, wd_ref,
                fg_ref, o_ref, h_ref, acc_ref, *, final):
    j = pl.program_id(1)
    last = pl.num_programs(1) - 1
    tm = x_ref.shape[0]

    def step(rs, is_first, is_last):
        if is_first:
            h_ref[rs, :] = _norm_modulate(
                x_ref[rs, :], g_ref[...], shift_ref[...], scale_ref[...]).astype(BF16)
        h = h_ref[rs, :]
        gate = jnp.dot(h, wg_ref[...], preferred_element_type=F32)
        up = jnp.dot(h, wu_ref[...], preferred_element_type=F32)
        a = (_silu(gate) * up).astype(BF16)
        down = jnp.dot(a, wd_ref[...], preferred_element_type=F32)
        if is_first:
            acc_ref[rs, :] = down
        elif is_last:
            y = x_ref[rs, :] + (0.5 * (1.0 + gate_ref[...])) * (acc_ref[rs, :] + down)
            if final:
                ms = jnp.mean(y * y, axis=-1, keepdims=True)
                y = y * lax.rsqrt(ms + EPS) * fg_ref[...]
            o_ref[rs, :] = y
        else:
            acc_ref[rs, :] += down

    edge_rows = _row_slices(tm, EDGE_SPLIT)

    @pl.when(j == 0)
    def _():
        for rs in edge_rows:
            step(rs, True, False)

    @pl.when(jnp.logical_and(j > 0, j < last))
    def _():
        step(slice(0, tm), False, False)

    @pl.when(j == last)
    def _():
        for rs in edge_rows:
            step(rs, False, True)


def _ffn(x2, mod5, norm_g4, wg, wu, wd, final_g, l, s, sub, final):
    rows, d = x2.shape
    f = wg.shape[-1]
    tm, tf = FFN_TM, FFN_TF
    tps = SEQ // tm
    return pl.pallas_call(
        functools.partial(_ffn_kernel, final=final),
        grid=(rows // tm, f // tf),
        in_specs=[
            pl.BlockSpec((tm, d), lambda i, j: (i, 0)),
            _mod_spec(l, sub, 0, tps), _mod_spec(l, sub, 1, tps), _mod_spec(l, sub, 2, tps),
            pl.BlockSpec((None, None, 1, d), lambda i, j: (l, sub, 0, 0)),
            pl.BlockSpec((None, None, d, tf), lambda i, j: (l, s, 0, j)),
            pl.BlockSpec((None, None, d, tf), lambda i, j: (l, s, 0, j)),
            pl.BlockSpec((None, None, tf, d), lambda i, j: (l, s, j, 0)),
            pl.BlockSpec((1, d), lambda i, j: (0, 0)),
        ],
        out_specs=pl.BlockSpec((tm, d), lambda i, j: (i, 0)),
        out_shape=jax.ShapeDtypeStruct((rows, d), F32),
        scratch_shapes=[pltpu.VMEM((tm, d), BF16), pltpu.VMEM((tm, d), F32)],
        compiler_params=_cparams("arbitrary", "arbitrary"),
        name="ffn",
    )(x2, mod5, mod5, mod5, norm_g4, wg, wu, wd, final_g)


def _causal_conv(buf_ref, w_ref, out_ref, *, taps, halo, rows, width):
    base = halo - (taps - 1)
    win = CONV_ROWS + halo
    for c0 in range(0, width, LANES):
        cs = slice(c0, c0 + LANES)
        wts = w_ref[:, cs]
        for r0 in range(0, rows, CONV_ROWS):
            window = buf_ref[r0:r0 + win, cs]
            acc = None
            for r in range(SUBLANES):
                ks = [k for k in range(taps) if (base + k) % SUBLANES == r]
                if not ks:
                    continue
                shifted = window if r == 0 else pltpu.roll(window, win - r, axis=0)
                for k in ks:
                    q = (base + k) // SUBLANES * SUBLANES
                    term = wts[k:k + 1, :] * shifted[q:q + CONV_ROWS, :]
                    acc = term if acc is None else acc + term
            out_ref[r0:r0 + CONV_ROWS, cs] = acc


def _even_kernel(x_ref, shift_ref, scale_ref, gate_ref, g_ref, win_ref, aconv_ref, lng_ref,
                 lnb_ref, bconv_ref, wout_ref, o_ref, abuf, cbuf, conv_ref, *, tiles_per_seq):
    tm = x_ref.shape[0]
    first = (pl.program_id(0) % tiles_per_seq) == 0

    @pl.when(first)
    def _():
        abuf[0:A_HALO, :] = jnp.zeros((A_HALO, A_WIDTH), F32)
        cbuf[0:B_HALO, :] = jnp.zeros((B_HALO, B_WIDTH), F32)

    @pl.when(jnp.logical_not(first))
    def _():
        abuf[0:A_HALO, :] = abuf[tm:tm + A_HALO, :]
        cbuf[0:B_HALO, :] = cbuf[tm:tm + B_HALO, :]

    x = x_ref[...]
    h = _norm_modulate(x, g_ref[...], shift_ref[...], scale_ref[...]).astype(BF16)

    def proj(lo, width):
        return jnp.dot(h, win_ref[:, lo:lo + width], preferred_element_type=F32)

    abuf[A_HALO:A_HALO + tm, :] = proj(0, A_WIDTH) * jax.nn.sigmoid(proj(A_WIDTH, A_WIDTH))
    lo = 2 * A_WIDTH
    cbuf[B_HALO:B_HALO + tm, :] = proj(lo + B_WIDTH, B_WIDTH) * proj(lo + 2 * B_WIDTH, B_WIDTH)
    bb = proj(lo, B_WIDTH)

    _causal_conv(abuf, aconv_ref, conv_ref, taps=A_KERNEL, halo=A_HALO, rows=tm, width=A_WIDTH)
    a = conv_ref[...]
    mu = jnp.mean(a, axis=-1, keepdims=True)
    ac = a - mu
    var = jnp.mean(ac * ac, axis=-1, keepdims=True)
    a = _silu(ac * lax.rsqrt(var + EPS) * lng_ref[...] + lnb_ref[...]).astype(BF16)
    y = jnp.dot(a, wout_ref[0:A_WIDTH, :], preferred_element_type=F32)

    _causal_conv(cbuf, bconv_ref, conv_ref, taps=B_KERNEL, halo=B_HALO, rows=tm, width=B_WIDTH)
    bo = (bb * conv_ref[...]).astype(BF16)
    y = y + jnp.dot(bo, wout_ref[A_WIDTH:A_WIDTH + B_WIDTH, :], preferred_element_type=F32)

    o_ref[...] = x + (1.0 + gate_ref[...]) * y


def _even_mix(x2, mod5, norm_g4, w_in, a_conv, ln_g, ln_b, b_conv, w_out, l, jx):
    rows, d = x2.shape
    tm = EVEN_TM
    tps = SEQ // tm
    in_w = w_in.shape[-1]
    resident = dict(pipeline_mode=pl.Buffered(1))
    return pl.pallas_call(
        functools.partial(_even_kernel, tiles_per_seq=tps),
        grid=(rows // tm,),
        in_specs=[
            pl.BlockSpec((tm, d), lambda i: (i, 0)),
            _mod_spec(l, 1, 0, tps), _mod_spec(l, 1, 1, tps), _mod_spec(l, 1, 2, tps),
            pl.BlockSpec((None, None, 1, d), lambda i: (l, 1, 0, 0)),
            pl.BlockSpec((None, d, in_w), lambda i: (jx, 0, 0), **resident),
            pl.BlockSpec((None, A_KERNEL, A_WIDTH), lambda i: (jx, 0, 0)),
            pl.BlockSpec((None, 1, A_WIDTH), lambda i: (jx, 0, 0)),
            pl.BlockSpec((None, 1, A_WIDTH), lambda i: (jx, 0, 0)),
            pl.BlockSpec((None, B_KERNEL, B_WIDTH), lambda i: (jx, 0, 0)),
            pl.BlockSpec((None, A_WIDTH + B_WIDTH, d), lambda i: (jx, 0, 0), **resident),
        ],
        out_specs=pl.BlockSpec((tm, d), lambda i: (i, 0)),
        out_shape=jax.ShapeDtypeStruct((rows, d), F32),
        scratch_shapes=[
            pltpu.VMEM((A_HALO + tm, A_WIDTH), F32),
            pltpu.VMEM((B_HALO + tm, B_WIDTH), F32),
            pltpu.VMEM((tm, max(A_WIDTH, B_WIDTH)), F32),
        ],
        compiler_params=_cparams("arbitrary"),
        name="even_mix",
    )(x2, mod5, mod5, mod5, norm_g4, w_in, a_conv, ln_g, ln_b, b_conv, w_out)


def _odd_kernel(x_ref, shift_ref, scale_ref, gate_ref, g_ref, win_ref, lng_ref, lnb_ref, ws_ref,
                bs_ref, wout_ref, o_ref, h_ref, v_ref, acc_ref, mu_ref, rstd_ref, *, nt):
    j = pl.program_id(1)
    tm = x_ref.shape[0]
    width = nt * win_ref.shape[1]
    edge_rows = _row_slices(tm, ODD_SPLIT)

    def in_proj(rs):
        return _gelu(jnp.dot(h_ref[rs, :], win_ref[...], preferred_element_type=F32))

    def row_stats(rs):
        total = jnp.zeros((rs.stop - rs.start, 1), F32)
        for t in range(nt):
            total = total + jnp.sum(v_ref[t, rs, :], axis=-1, keepdims=True)
        mu = total * (1.0 / width)
        sq = jnp.zeros_like(mu)
        for t in range(nt):
            vc = v_ref[t, rs, :] - mu
            sq = sq + jnp.sum(vc * vc, axis=-1, keepdims=True)
        mu_ref[rs, :] = mu
        rstd_ref[rs, :] = lax.rsqrt(sq * (1.0 / width) + EPS)

    @pl.when(j == 0)
    def _():
        for rs in edge_rows:
            h_ref[rs, :] = _norm_modulate(
                x_ref[rs, :], g_ref[...], shift_ref[...], scale_ref[...]).astype(BF16)
            v_ref[0, rs, :] = in_proj(rs)
            acc_ref[rs, :] = jnp.zeros((rs.stop - rs.start, acc_ref.shape[1]), F32)

    @pl.when(jnp.logical_and(j > 0, j < nt - 1))
    def _():
        v_ref[j] = in_proj(slice(0, tm))

    @pl.when(j == nt - 1)
    def _():
        for rs in edge_rows:
            v_ref[nt - 1, rs, :] = in_proj(rs)
            row_stats(rs)

    def gated_step(is_last):
        t = j - nt
        row = lax.broadcasted_iota(jnp.int32, (CHUNK, CHUNK), 0)
        col = lax.broadcasted_iota(jnp.int32, (CHUNK, CHUNK), 1)
        heads = ws_ref.shape[0]
        ws = [jnp.where(col <= row, ws_ref[hh], 0.0).astype(BF16) for hh in range(heads)]
        for rs in edge_rows:
            u = in_proj(rs)
            vn = ((v_ref[t, rs, :] - mu_ref[rs, :]) * rstd_ref[rs, :] * lng_ref[...]
                  + lnb_ref[...]).astype(BF16)
            gated = []
            for c0 in range(0, rs.stop - rs.start, CHUNK):
                cr = slice(c0, c0 + CHUNK)
                parts = []
                for hh in range(heads):
                    cs = slice(hh * C_HEAD_DIM, (hh + 1) * C_HEAD_DIM)
                    vs = jnp.dot(ws[hh], vn[cr, cs], preferred_element_type=F32) + bs_ref[hh]
                    parts.append((u[cr, cs] * vs).astype(BF16))
                gated.append(jnp.concatenate(parts, axis=1))
            gated = jnp.concatenate(gated, axis=0) if len(gated) > 1 else gated[0]
            y = jnp.dot(gated, wout_ref[...], preferred_element_type=F32)
            if is_last:
                o_ref[rs, :] = x_ref[rs, :] + (1.0 + gate_ref[...]) * (acc_ref[rs, :] + y)
            else:
                acc_ref[rs, :] += y

    @pl.when(jnp.logical_and(j >= nt, j < 2 * nt - 1))
    def _():
        gated_step(False)

    @pl.when(j == 2 * nt - 1)
    def _():
        gated_step(True)


def _odd_mix(x2, mod5, norm_g4, w_in, ln_g, ln_b, w_s, b_s4, w_out, l, jx):
    rows, d = x2.shape
    tm, tn = ODD_TM, ODD_TN
    tps = SEQ // tm
    nt = C_WIDTH // tn
    hpt = tn // C_HEAD_DIM

    def u_tile(j):
        return jnp.maximum(j - nt, 0)

    return pl.pallas_call(
        functools.partial(_odd_kernel, nt=nt),
        grid=(rows // tm, 2 * nt),
        in_specs=[
            pl.BlockSpec((tm, d), lambda i, j: (i, 0)),
            _mod_spec(l, 1, 0, tps), _mod_spec(l, 1, 1, tps), _mod_spec(l, 1, 2, tps),
            pl.BlockSpec((None, None, 1, d), lambda i, j: (l, 1, 0, 0)),
            pl.BlockSpec((None, d, tn), lambda i, j: (jx, 0, jnp.where(j < nt, j + nt, j - nt))),
            pl.BlockSpec((None, 1, tn), lambda i, j: (jx, 0, u_tile(j))),
            pl.BlockSpec((None, 1, tn), lambda i, j: (jx, 0, u_tile(j))),
            pl.BlockSpec((None, hpt, CHUNK, CHUNK), lambda i, j: (jx, u_tile(j), 0, 0)),
            pl.BlockSpec((None, hpt, CHUNK, 1), lambda i, j: (jx, u_tile(j), 0, 0)),
            pl.BlockSpec((None, tn, d), lambda i, j: (jx, u_tile(j), 0)),
        ],
        out_specs=pl.BlockSpec((tm, d), lambda i, j: (i, 0)),
        out_shape=jax.ShapeDtypeStruct((rows, d), F32),
        scratch_shapes=[
            pltpu.VMEM((tm, d), BF16),
            pltpu.VMEM((nt, tm, tn), F32),
            pltpu.VMEM((tm, d), F32),
            pltpu.VMEM((tm, 1), F32),
            pltpu.VMEM((tm, 1), F32),
        ],
        compiler_params=_cparams("arbitrary", "arbitrary"),
        name="odd_mix",
    )(x2, mod5, mod5, mod5, norm_g4, w_in, ln_g, ln_b, w_s, b_s4, w_out)


def kernel(x, c, ada_w, ada_b, norm_g, ffn_w_gate, ffn_w_up, ffn_w_down, ev_w_in, ev_a_conv,
           ev_a_ln_g, ev_a_ln_b, ev_b_conv, ev_w_out, od_w_in, od_v_ln_g, od_v_ln_b, od_w_s,
           od_b_s, od_w_out, final_g):
    bsz, seq, d = x.shape
    depth = ada_w.shape[0]
    assert (seq, d) == (SEQ, D_MODEL)

    mod5 = _modulation(c, ada_w, ada_b).reshape(depth, bsz, N_SUB * 3, 1, d)
    norm_g4 = norm_g.reshape(depth, N_SUB, 1, d)
    final_g2 = final_g.reshape(1, d)
    wg, wu, wd = (w.astype(BF16) for w in (ffn_w_gate, ffn_w_up, ffn_w_down))
    ev_in, ev_out = ev_w_in.astype(BF16), ev_w_out.astype(BF16)
    od_in, od_out = od_w_in.astype(BF16), od_w_out.astype(BF16)
    ev_ln_g, ev_ln_b = (v.reshape(-1, 1, A_WIDTH) for v in (ev_a_ln_g, ev_a_ln_b))
    od_ln_g, od_ln_b = (v.reshape(-1, 1, C_WIDTH) for v in (od_v_ln_g, od_v_ln_b))
    od_bs = od_b_s.reshape(od_b_s.shape[0], -1, CHUNK, 1)

    x2 = x.reshape(bsz * seq, d)
    for l in range(depth):
        x2 = _ffn(x2, mod5, norm_g4, wg, wu, wd, final_g2, l, 0, 0, False)
        jx = l // 2
        if l % 2 == 0:
            x2 = _even_mix(x2, mod5, norm_g4, ev_in, ev_a_conv, ev_ln_g, ev_ln_b, ev_b_conv,
                           ev_out, l, jx)
        else:
            x2 = _odd_mix(x2, mod5, norm_g4, od_in, od_ln_g, od_ln_b, od_w_s, od_bs, od_out,
                          l, jx)
        x2 = _ffn(x2, mod5, norm_g4, wg, wu, wd, final_g2, l, 1, 2, l == depth - 1)
    return x2.reshape(bsz, seq, d)
```

```python
import functools

import jax
import jax.numpy as jnp
from jax import lax
from jax.experimental import pallas as pl
from jax.experimental.pallas import tpu as pltpu

D_MODEL = 2048
SEQ = 2048
FFN_HIDDEN = 5632
A_WIDTH = 1024
B_WIDTH = 1024
A_KERNEL = 31
B_KERNEL = 3
C_WIDTH = 4096
C_HEAD_DIM = 256
CHUNK = 128
N_SUB = 3
EPS = 1e-6

F32 = jnp.float32
BF16 = jnp.bfloat16

SUBLANES = 8
LANES = 128
A_HALO = 32
B_HALO = 8
CONV_ROWS = 64
VMEM_LIMIT = 56 * 1024 * 1024

FFN_TM = 1024
FFN_TF = 512
FFN_SPLIT = 2
EDGE_SPLIT = 2
EVEN_TM = 256
EVEN_SPLIT = 1
ODD_TM = 512
ODD_TN = 1024
ODD_SPLIT = 2
MOD_TN = 1024


def _cparams(*sem):
    return pltpu.CompilerParams(dimension_semantics=sem, vmem_limit_bytes=VMEM_LIMIT)


def _silu(v):
    return v * jax.nn.sigmoid(v)


def _gelu(v):
    return 0.5 * v * (1.0 + lax.erf(v * (0.5 ** 0.5)))


def _norm_modulate(x, g, shift, scale):
    ms = jnp.mean(x * x, axis=-1, keepdims=True)
    y = x * lax.rsqrt(ms + EPS) * g
    return y * (1.0 + scale) + shift


def _row_slices(rows, parts):
    sub = rows // parts
    return [slice(r * sub, (r + 1) * sub) for r in range(parts)]


def _mod_kernel(c_ref, w_ref, b_ref, o_ref):
    cond = _silu(c_ref[...]).astype(BF16)
    o_ref[...] = jnp.dot(cond, w_ref[...].astype(BF16), preferred_element_type=F32) + b_ref[...]


def _modulation(c, ada_w, ada_b):
    depth, d, n = ada_w.shape
    bsz = c.shape[0]
    tn = MOD_TN
    return pl.pallas_call(
        _mod_kernel,
        grid=(depth, n // tn),
        in_specs=[
            pl.BlockSpec((bsz, d), lambda l, j: (0, 0)),
            pl.BlockSpec((None, d, tn), lambda l, j: (l, 0, j)),
            pl.BlockSpec((None, 1, tn), lambda l, j: (l, 0, j)),
        ],
        out_specs=pl.BlockSpec((None, bsz, tn), lambda l, j: (l, 0, j)),
        out_shape=jax.ShapeDtypeStruct((depth, bsz, n), F32),
        compiler_params=_cparams("arbitrary", "arbitrary"),
        name="modulation",
    )(c, ada_w, ada_b.reshape(depth, 1, n))


def _mod_spec(l, sub, kind, rows_per_batch_tile):
    k = sub * 3 + kind
    return pl.BlockSpec((None, None, None, 1, D_MODEL),
                        lambda i, *_: (l, i // rows_per_batch_tile, k, 0, 0))


def _ffn_kernel(*refs, final, cast_next):
    (x_hbm, shift_ref, scale_ref, gate_ref, g_ref, wg_ref, wu_ref, wd_ref, fg_ref) = refs[:9]
    refs = refs[9:]
    if cast_next:
        (ng_ref, nu_ref, nd_ref, o_hbm, cg_ref, cu_ref, cd_ref) = refs[:7]
        refs = refs[7:]
    else:
        o_hbm, refs = refs[0], refs[1:]
    xbuf, h_ref, acc_ref, in_sem, out_sem = refs

    i = pl.program_id(0)
    j = pl.program_id(1)
    n_tiles = pl.num_programs(0)
    last = pl.num_programs(1) - 1
    tm = xbuf.shape[1]
    slot = i % 2

    def in_copy(tile, s):
        return pltpu.make_async_copy(
            x_hbm.at[pl.ds(tile * tm, tm), :], xbuf.at[s], in_sem.at[s])

    def out_copy(tile, s):
        return pltpu.make_async_copy(
            xbuf.at[s], o_hbm.at[pl.ds(tile * tm, tm), :], out_sem.at[s])

    @pl.when(jnp.logical_and(i == 0, j == 0))
    def _():
        in_copy(0, 0).start()

    @pl.when(j == 0)
    def _():
        in_copy(i, slot).wait()

    @pl.when(j == 1)
    def _():
        @pl.when(i >= 1)
        def _():
            out_copy(i - 1, 1 - slot).wait()

        @pl.when(i + 1 < n_tiles)
        def _():
            in_copy(i + 1, 1 - slot).start()

    if cast_next:
        cg_ref[...] = ng_ref[...].astype(BF16)
        cu_ref[...] = nu_ref[...].astype(BF16)
        cd_ref[...] = nd_ref[...].astype(BF16)

    def step(rs, is_first, is_last):
        if is_first:
            h_ref[rs, :] = _norm_modulate(
                xbuf[slot, rs, :], g_ref[...], shift_ref[...], scale_ref[...]).astype(BF16)
        h = h_ref[rs, :]
        gate = jnp.dot(h, wg_ref[...], preferred_element_type=F32)
        up = jnp.dot(h, wu_ref[...], preferred_element_type=F32)
        a = (_silu(gate) * up).astype(BF16)
        down = jnp.dot(a, wd_ref[...], preferred_element_type=F32)
        if is_first:
            acc_ref[rs, :] = down
        elif is_last:
            y = xbuf[slot, rs, :] + (0.5 * (1.0 + gate_ref[...])) * (acc_ref[rs, :] + down)
            if final:
                ms = jnp.mean(y * y, axis=-1, keepdims=True)
                y = y * lax.rsqrt(ms + EPS) * fg_ref[...]
            xbuf[slot, rs, :] = y
        else:
            acc_ref[rs, :] += down

    edge_rows = _row_slices(tm, FFN_SPLIT * EDGE_SPLIT)

    @pl.when(j == 0)
    def _():
        for rs in edge_rows:
            step(rs, True, False)

    @pl.when(jnp.logical_and(j > 0, j < last))
    def _():
        for rs in _row_slices(tm, FFN_SPLIT):
            step(rs, False, False)

    @pl.when(j == last)
    def _():
        for rs in edge_rows:
            step(rs, False, True)
        out_copy(i, slot).start()

        @pl.when(i == n_tiles - 1)
        def _():
            out_copy(i, slot).wait()


def _ffn(x2, mod5, norm_g4, wg, wu, wd, final_g, l, sub, final, nxt):
    rows, d = x2.shape
    f = wg.shape[-1]
    tm, tf = FFN_TM, FFN_TF
    tps = SEQ // tm
    n_i, n_j = rows // tm, f // tf
    assert n_j >= 2, "the ring's hand-over runs at hidden tile 1"
    in_specs = [
        pl.BlockSpec(memory_space=pl.ANY),
        _mod_spec(l, sub, 0, tps), _mod_spec(l, sub, 1, tps), _mod_spec(l, sub, 2, tps),
        pl.BlockSpec((None, None, 1, d), lambda i, j: (l, sub, 0, 0)),
        pl.BlockSpec((d, tf), lambda i, j: (0, j)),
        pl.BlockSpec((d, tf), lambda i, j: (0, j)),
        pl.BlockSpec((tf, d), lambda i, j: (j, 0)),
        pl.BlockSpec((1, d), lambda i, j: (0, 0)),
    ]
    operands = [x2, mod5, mod5, mod5, norm_g4, wg, wu, wd, final_g]
    out_specs = [pl.BlockSpec(memory_space=pl.ANY)]
    out_shape = [jax.ShapeDtypeStruct((rows, d), F32)]
    if nxt is not None:
        ng, nu, nd, nl, ns = nxt
        gr, dr = d // n_i, f // (n_i * n_j)
        in_specs += [
            pl.BlockSpec((None, None, gr, tf), lambda i, j: (nl, ns, i, j)),
            pl.BlockSpec((None, None, gr, tf), lambda i, j: (nl, ns, i, j)),
            pl.BlockSpec((None, None, dr, d), lambda i, j: (nl, ns, i * n_j + j, 0)),
        ]
        operands += [ng, nu, nd]
        out_specs += [
            pl.BlockSpec((gr, tf), lambda i, j: (i, j)),
            pl.BlockSpec((gr, tf), lambda i, j: (i, j)),
            pl.BlockSpec((dr, d), lambda i, j: (i * n_j + j, 0)),
        ]
        out_shape += [jax.ShapeDtypeStruct((d, f), BF16), jax.ShapeDtypeStruct((d, f), BF16),
                      jax.ShapeDtypeStruct((f, d), BF16)]
    outs = pl.pallas_call(
        functools.partial(_ffn_kernel, final=final, cast_next=nxt is not None),
        grid=(n_i, n_j),
        in_specs=in_specs,
        out_specs=out_specs,
        out_shape=out_shape,
        scratch_shapes=[
            pltpu.VMEM((2, tm, d), F32),
            pltpu.VMEM((tm, d), BF16),
            pltpu.VMEM((tm, d), F32),
            pltpu.SemaphoreType.DMA((2,)),
            pltpu.SemaphoreType.DMA((2,)),
        ],
        compiler_params=_cparams("arbitrary", "arbitrary"),
        name="ffn",
    )(*operands)
    return outs[0], tuple(outs[1:])


def _causal_conv(buf_ref, w_ref, out_ref, *, taps, halo, rows, width):
    base = halo - (taps - 1)
    win = CONV_ROWS + halo
    for c0 in range(0, width, LANES):
        cs = slice(c0, c0 + LANES)
        wts = w_ref[:, cs]
        for r0 in range(rows.start, rows.stop, CONV_ROWS):
            window = buf_ref[r0:r0 + win, cs]
            acc = None
            for r in range(SUBLANES):
                ks = [k for k in range(taps) if (base + k) % SUBLANES == r]
                if not ks:
                    continue
                shifted = window if r == 0 else pltpu.roll(window, win - r, axis=0)
                for k in ks:
                    q = (base + k) // SUBLANES * SUBLANES
                    term = wts[k:k + 1, :] * shifted[q:q + CONV_ROWS, :]
                    acc = term if acc is None else acc + term
            out_ref[r0:r0 + CONV_ROWS, cs] = acc


def _even_kernel(x_ref, shift_ref, scale_ref, gate_ref, g_ref, win_ref, aconv_ref, lng_ref,
                 lnb_ref, bconv_ref, wout_ref, o_ref, abuf, cbuf, conv_ref, *, tiles_per_seq):
    tm = x_ref.shape[0]
    first = (pl.program_id(0) % tiles_per_seq) == 0

    @pl.when(first)
    def _():
        abuf[0:A_HALO, :] = jnp.zeros((A_HALO, A_WIDTH), F32)
        cbuf[0:B_HALO, :] = jnp.zeros((B_HALO, B_WIDTH), F32)

    @pl.when(jnp.logical_not(first))
    def _():
        abuf[0:A_HALO, :] = abuf[tm:tm + A_HALO, :]
        cbuf[0:B_HALO, :] = cbuf[tm:tm + B_HALO, :]

    for rs in _row_slices(tm, EVEN_SPLIT):
        x = x_ref[rs, :]
        h = _norm_modulate(x, g_ref[...], shift_ref[...], scale_ref[...]).astype(BF16)

        def proj(lo, width, h=h):
            return jnp.dot(h, win_ref[:, lo:lo + width], preferred_element_type=F32)

        abuf[A_HALO + rs.start:A_HALO + rs.stop, :] = (
            proj(0, A_WIDTH) * jax.nn.sigmoid(proj(A_WIDTH, A_WIDTH)))
        lo = 2 * A_WIDTH
        cbuf[B_HALO + rs.start:B_HALO + rs.stop, :] = (
            proj(lo + B_WIDTH, B_WIDTH) * proj(lo + 2 * B_WIDTH, B_WIDTH))
        bb = proj(lo, B_WIDTH)

        _causal_conv(abuf, aconv_ref, conv_ref, taps=A_KERNEL, halo=A_HALO, rows=rs,
                     width=A_WIDTH)
        a = conv_ref[rs, :]
        mu = jnp.mean(a, axis=-1, keepdims=True)
        ac = a - mu
        var = jnp.mean(ac * ac, axis=-1, keepdims=True)
        a = _silu(ac * lax.rsqrt(var + EPS) * lng_ref[...] + lnb_ref[...]).astype(BF16)
        y = jnp.dot(a, wout_ref[0:A_WIDTH, :], preferred_element_type=F32)

        _causal_conv(cbuf, bconv_ref, conv_ref, taps=B_KERNEL, halo=B_HALO, rows=rs,
                     width=B_WIDTH)
        bo = (bb * conv_ref[rs, :]).astype(BF16)
        y = y + jnp.dot(bo, wout_ref[A_WIDTH:A_WIDTH + B_WIDTH, :], preferred_element_type=F32)

        o_ref[rs, :] = x + (1.0 + gate_ref[...]) * y


def _even_mix(x2, mod5, norm_g4, w_in, a_conv, ln_g, ln_b, b_conv, w_out, l, jx):
    rows, d = x2.shape
    tm = EVEN_TM
    tps = SEQ // tm
    in_w = w_in.shape[-1]
    resident = dict(pipeline_mode=pl.Buffered(1))
    return pl.pallas_call(
        functools.partial(_even_kernel, tiles_per_seq=tps),
        grid=(rows // tm,),
        in_specs=[
            pl.BlockSpec((tm, d), lambda i: (i, 0)),
            _mod_spec(l, 1, 0, tps), _mod_spec(l, 1, 1, tps), _mod_spec(l, 1, 2, tps),
            pl.BlockSpec((None, None, 1, d), lambda i: (l, 1, 0, 0)),
            pl.BlockSpec((None, d, in_w), lambda i: (jx, 0, 0), **resident),
            pl.BlockSpec((None, A_KERNEL, A_WIDTH), lambda i: (jx, 0, 0)),
            pl.BlockSpec((None, 1, A_WIDTH), lambda i: (jx, 0, 0)),
            pl.BlockSpec((None, 1, A_WIDTH), lambda i: (jx, 0, 0)),
            pl.BlockSpec((None, B_KERNEL, B_WIDTH), lambda i: (jx, 0, 0)),
            pl.BlockSpec((None, A_WIDTH + B_WIDTH, d), lambda i: (jx, 0, 0), **resident),
        ],
        out_specs=pl.BlockSpec((tm, d), lambda i: (i, 0)),
        out_shape=jax.ShapeDtypeStruct((rows, d), F32),
        scratch_shapes=[
            pltpu.VMEM((A_HALO + tm, A_WIDTH), F32),
            pltpu.VMEM((B_HALO + tm, B_WIDTH), F32),
            pltpu.VMEM((tm, max(A_WIDTH, B_WIDTH)), F32),
        ],
        compiler_params=_cparams("arbitrary"),
        name="even_mix",
    )(x2, mod5, mod5, mod5, norm_g4, w_in, a_conv, ln_g, ln_b, b_conv, w_out)


def _odd_kernel(x_ref, shift_ref, scale_ref, gate_ref, g_ref, win_ref, lng_ref, lnb_ref, ws_ref,
                bs_ref, wout_ref, o_ref, h_ref, v_ref, acc_ref, mu_ref, rstd_ref, *, nt):
    j = pl.program_id(1)
    tm = x_ref.shape[0]
    width = nt * win_ref.shape[1]
    edge_rows = _row_slices(tm, ODD_SPLIT)

    def in_proj(rs):
        return _gelu(jnp.dot(h_ref[rs, :], win_ref[...], preferred_element_type=F32))

    def row_stats(rs):
        total = jnp.zeros((rs.stop - rs.start, 1), F32)
        for t in range(nt):
            total = total + jnp.sum(v_ref[t, rs, :], axis=-1, keepdims=True)
        mu = total * (1.0 / width)
        sq = jnp.zeros_like(mu)
        for t in range(nt):
            vc = v_ref[t, rs, :] - mu
            sq = sq + jnp.sum(vc * vc, axis=-1, keepdims=True)
        mu_ref[rs, :] = mu
        rstd_ref[rs, :] = lax.rsqrt(sq * (1.0 / width) + EPS)

    @pl.when(j == 0)
    def _():
        for rs in edge_rows:
            h_ref[rs, :] = _norm_modulate(
                x_ref[rs, :], g_ref[...], shift_ref[...], scale_ref[...]).astype(BF16)
            v_ref[0, rs, :] = in_proj(rs)
            acc_ref[rs, :] = jnp.zeros((rs.stop - rs.start, acc_ref.shape[1]), F32)

    @pl.when(jnp.logical_and(j > 0, j < nt - 1))
    def _():
        v_ref[j] = in_proj(slice(0, tm))

    @pl.when(j == nt - 1)
    def _():
        for rs in edge_rows:
            v_ref[nt - 1, rs, :] = in_proj(rs)
            row_stats(rs)

    def gated_step(is_last):
        t = j - nt
        row = lax.broadcasted_iota(jnp.int32, (CHUNK, CHUNK), 0)
        col = lax.broadcasted_iota(jnp.int32, (CHUNK, CHUNK), 1)
        heads = ws_ref.shape[0]
        ws = [jnp.where(col <= row, ws_ref[hh], 0.0).astype(BF16) for hh in range(heads)]
        for rs in edge_rows:
            u = in_proj(rs)
            vn = ((v_ref[t, rs, :] - mu_ref[rs, :]) * rstd_ref[rs, :] * lng_ref[...]
                  + lnb_ref[...]).astype(BF16)
            gated = []
            for c0 in range(0, rs.stop - rs.start, CHUNK):
                cr = slice(c0, c0 + CHUNK)
                parts = []
                for hh in range(heads):
                    cs = slice(hh * C_HEAD_DIM, (hh + 1) * C_HEAD_DIM)
                    vs = jnp.dot(ws[hh], vn[cr, cs], preferred_element_type=F32) + bs_ref[hh]
                    parts.append((u[cr, cs] * vs).astype(BF16))
                gated.append(jnp.concatenate(parts, axis=1))
            gated = jnp.concatenate(gated, axis=0) if len(gated) > 1 else gated[0]
            y = jnp.dot(gated, wout_ref[...], preferred_element_type=F32)
            if is_last:
                o_ref[rs, :] = x_ref[rs, :] + (1.0 + gate_ref[...]) * (acc_ref[rs, :] + y)
            else:
                acc_ref[rs, :] += y

    @pl.when(jnp.logical_and(j >= nt, j < 2 * nt - 1))
    def _():
        gated_step(False)

    @pl.when(j == 2 * nt - 1)
    def _():
        gated_step(True)


def _odd_mix(x2, mod5, norm_g4, w_in, ln_g, ln_b, w_s, b_s4, w_out, l, jx):
    rows, d = x2.shape
    tm, tn = ODD_TM, ODD_TN
    tps = SEQ // tm
    nt = C_WIDTH // tn
    hpt = tn // C_HEAD_DIM

    def u_tile(j):
        return jnp.maximum(j - nt, 0)

    return pl.pallas_call(
        functools.partial(_odd_kernel, nt=nt),
        grid=(rows // tm, 2 * nt),
        in_specs=[
            pl.BlockSpec((tm, d), lambda i, j: (i, 0)),
            _mod_spec(l, 1, 0, tps), _mod_spec(l, 1, 1, tps), _mod_spec(l, 1, 2, tps),
            pl.BlockSpec((None, None, 1, d), lambda i, j: (l, 1, 0, 0)),
            pl.BlockSpec((None, d, tn), lambda i, j: (jx, 0, jnp.where(j < nt, j + nt, j - nt))),
            pl.BlockSpec((None, 1, tn), lambda i, j: (jx, 0, u_tile(j))),
            pl.BlockSpec((None, 1, tn), lambda i, j: (jx, 0, u_tile(j))),
            pl.BlockSpec((None, hpt, CHUNK, CHUNK), lambda i, j: (jx, u_tile(j), 0, 0)),
            pl.BlockSpec((None, hpt, CHUNK, 1), lambda i, j: (jx, u_tile(j), 0, 0)),
            pl.BlockSpec((None, tn, d), lambda i, j: (jx, u_tile(j), 0)),
        ],
        out_specs=pl.BlockSpec((tm, d), lambda i, j: (i, 0)),
        out_shape=jax.ShapeDtypeStruct((rows, d), F32),
        scratch_shapes=[
            pltpu.VMEM((tm, d), BF16),
            pltpu.VMEM((nt, tm, tn), F32),
            pltpu.VMEM((tm, d), F32),
            pltpu.VMEM((tm, 1), F32),
            pltpu.VMEM((tm, 1), F32),
        ],
        compiler_params=_cparams("arbitrary", "arbitrary"),
        name="odd_mix",
    )(x2, mod5, mod5, mod5, norm_g4, w_in, ln_g, ln_b, w_s, b_s4, w_out)


def kernel(x, c, ada_w, ada_b, norm_g, ffn_w_gate, ffn_w_up, ffn_w_down, ev_w_in, ev_a_conv,
           ev_a_ln_g, ev_a_ln_b, ev_b_conv, ev_w_out, od_w_in, od_v_ln_g, od_v_ln_b, od_w_s,
           od_b_s, od_w_out, final_g):
    bsz, seq, d = x.shape
    depth = ada_w.shape[0]
    assert (seq, d) == (SEQ, D_MODEL)

    mod5 = _modulation(c, ada_w, ada_b).reshape(depth, bsz, N_SUB * 3, 1, d)
    norm_g4 = norm_g.reshape(depth, N_SUB, 1, d)
    final_g2 = final_g.reshape(1, d)
    ffn_w = (ffn_w_gate, ffn_w_up, ffn_w_down)
    w_bf16 = tuple(w[0, 0].astype(BF16) for w in ffn_w)
    ev_in, ev_out = ev_w_in.astype(BF16), ev_w_out.astype(BF16)
    od_in, od_out = od_w_in.astype(BF16), od_w_out.astype(BF16)
    ev_ln_g, ev_ln_b = (v.reshape(-1, 1, A_WIDTH) for v in (ev_a_ln_g, ev_a_ln_b))
    od_ln_g, od_ln_b = (v.reshape(-1, 1, C_WIDTH) for v in (od_v_ln_g, od_v_ln_b))
    od_bs = od_b_s.reshape(od_b_s.shape[0], -1, CHUNK, 1)

    x2 = x.reshape(bsz * seq, d)
    for l in range(depth):
        x2, w_bf16 = _ffn(x2, mod5, norm_g4, *w_bf16, final_g2, l, 0, False, ffn_w + (l, 1))
        jx = l // 2
        if l % 2 == 0:
            x2 = _even_mix(x2, mod5, norm_g4, ev_in, ev_a_conv, ev_ln_g, ev_ln_b, ev_b_conv,
                           ev_out, l, jx)
        else:
            x2 = _odd_mix(x2, mod5, norm_g4, od_in, od_ln_g, od_ln_b, od_w_s, od_bs, od_out,
                          l, jx)
        is_last = l == depth - 1
        x2, w_bf16 = _ffn(x2, mod5, norm_g4, *w_bf16, final_g2, l, 2, is_last,
                          None if is_last else ffn_w + (l + 1, 0))
    return x2.reshape(bsz, seq, d)
```

```python
import functools

import jax
import jax.numpy as jnp
from jax import lax
from jax.experimental import pallas as pl
from jax.experimental.pallas import tpu as pltpu

D_MODEL = 2048
SEQ = 2048
FFN_HIDDEN = 5632
A_WIDTH = 1024
B_WIDTH = 1024
A_KERNEL = 31
B_KERNEL = 3
C_WIDTH = 4096
C_HEAD_DIM = 256
CHUNK = 128
N_SUB = 3
EPS = 1e-6

F32 = jnp.float32
BF16 = jnp.bfloat16

SUBLANES = 8
LANES = 128
A_HALO = 32
B_HALO = 8
CONV_ROWS = 32
VMEM_LIMIT = 56 * 1024 * 1024

FFN_TM = 1024
FFN_TF = 512
FFN_SPLIT = 2
EDGE_SPLIT = 2
EVEN_TM = 256
EVEN_CG = 256
ODD_TM = 512
ODD_TN = 1024
ODD_SPLIT = 2
MOD_TN = 1024


def _cparams(*sem):
    return pltpu.CompilerParams(dimension_semantics=sem, vmem_limit_bytes=VMEM_LIMIT)


def _silu(v):
    return v * jax.nn.sigmoid(v)


def _gelu(v):
    return 0.5 * v * (1.0 + lax.erf(v * (0.5 ** 0.5)))


def _norm_modulate(x, g, shift, scale):
    ms = jnp.mean(x * x, axis=-1, keepdims=True)
    y = x * lax.rsqrt(ms + EPS) * g
    return y * (1.0 + scale) + shift


def _cast_kernel(w_ref, o_ref):
    o_ref[...] = w_ref[...].astype(BF16)


def _tile_major(w, lead, tn, ntiles=None):
    d, n = w.shape[-2:]
    ntiles = n // tn if ntiles is None else ntiles
    return pl.pallas_call(
        _cast_kernel,
        grid=(ntiles,),
        in_specs=[pl.BlockSpec((None,) * len(lead) + (d, tn), lambda t: (*lead, 0, t))],
        out_specs=pl.BlockSpec((None, d, tn), lambda t: (t, 0, 0)),
        out_shape=jax.ShapeDtypeStruct((ntiles, d, tn), BF16),
        compiler_params=_cparams("arbitrary"),
        name="cast_tiles",
    )(w)


def _row_slices(rows, parts):
    sub = rows // parts
    return [slice(r * sub, (r + 1) * sub) for r in range(parts)]


def _mod_kernel(c_ref, w_ref, b_ref, o_ref):
    cond = _silu(c_ref[...]).astype(BF16)
    o_ref[...] = jnp.dot(cond, w_ref[...].astype(BF16), preferred_element_type=F32) + b_ref[...]


def _modulation(c, ada_w, ada_b):
    depth, d, n = ada_w.shape
    bsz = c.shape[0]
    tn = MOD_TN
    return pl.pallas_call(
        _mod_kernel,
        grid=(depth, n // tn),
        in_specs=[
            pl.BlockSpec((bsz, d), lambda l, j: (0, 0)),
            pl.BlockSpec((None, d, tn), lambda l, j: (l, 0, j)),
            pl.BlockSpec((None, 1, tn), lambda l, j: (l, 0, j)),
        ],
        out_specs=pl.BlockSpec((None, bsz, tn), lambda l, j: (l, 0, j)),
        out_shape=jax.ShapeDtypeStruct((depth, bsz, n), F32),
        compiler_params=_cparams("arbitrary", "arbitrary"),
        name="modulation",
    )(c, ada_w, ada_b.reshape(depth, 1, n))


def _mod_spec(l, sub, kind, rows_per_batch_tile):
    k = sub * 3 + kind
    return pl.BlockSpec((None, None, None, 1, D_MODEL),
                        lambda i, *_: (l, i // rows_per_batch_tile, k, 0, 0))


def _ffn_kernel(*refs, final, cast_next):
    (x_hbm, shift_ref, scale_ref, gate_ref, g_ref, wg_ref, wu_ref, wd_ref, fg_ref) = refs[:9]
    refs = refs[9:]
    if cast_next:
        (ng_ref, nu_ref, nd_ref, o_hbm, cg_ref, cu_ref, cd_ref) = refs[:7]
        refs = refs[7:]
    else:
        o_hbm, refs = refs[0], refs[1:]
    xbuf, h_ref, acc_ref, in_sem, out_sem = refs

    i = pl.program_id(0)
    j = pl.program_id(1)
    n_tiles = pl.num_programs(0)
    last = pl.num_programs(1) - 1
    tm = xbuf.shape[1]
    slot = i % 2

    def in_copy(tile, s):
        return pltpu.make_async_copy(
            x_hbm.at[pl.ds(tile * tm, tm), :], xbuf.at[s], in_sem.at[s])

    def out_copy(tile, s):
        return pltpu.make_async_copy(
            xbuf.at[s], o_hbm.at[pl.ds(tile * tm, tm), :], out_sem.at[s])

    @pl.when(jnp.logical_and(i == 0, j == 0))
    def _():
        in_copy(0, 0).start()

    @pl.when(j == 0)
    def _():
        in_copy(i, slot).wait()

    @pl.when(j == 1)
    def _():
        @pl.when(i >= 1)
        def _():
            out_copy(i - 1, 1 - slot).wait()

        @pl.when(i + 1 < n_tiles)
        def _():
            in_copy(i + 1, 1 - slot).start()

    if cast_next:
        cg_ref[...] = ng_ref[...].astype(BF16)
        cu_ref[...] = nu_ref[...].astype(BF16)
        cd_ref[...] = nd_ref[...].astype(BF16)

    def step(rs, is_first, is_last):
        if is_first:
            h_ref[rs, :] = _norm_modulate(
                xbuf[slot, rs, :], g_ref[...], shift_ref[...], scale_ref[...]).astype(BF16)
        h = h_ref[rs, :]
        gate = jnp.dot(h, wg_ref[...], preferred_element_type=F32)
        up = jnp.dot(h, wu_ref[...], preferred_element_type=F32)
        a = (_silu(gate) * up).astype(BF16)
        down = jnp.dot(a, wd_ref[...], preferred_element_type=F32)
        if is_first:
            acc_ref[rs, :] = down
        elif is_last:
            y = xbuf[slot, rs, :] + (0.5 * (1.0 + gate_ref[...])) * (acc_ref[rs, :] + down)
            if final:
                ms = jnp.mean(y * y, axis=-1, keepdims=True)
                y = y * lax.rsqrt(ms + EPS) * fg_ref[...]
            xbuf[slot, rs, :] = y
        else:
            acc_ref[rs, :] += down

    edge_rows = _row_slices(tm, FFN_SPLIT * EDGE_SPLIT)

    @pl.when(j == 0)
    def _():
        for rs in edge_rows:
            step(rs, True, False)

    @pl.when(jnp.logical_and(j > 0, j < last))
    def _():
        for rs in _row_slices(tm, FFN_SPLIT):
            step(rs, False, False)

    @pl.when(j == last)
    def _():
        for rs in edge_rows:
            step(rs, False, True)
        out_copy(i, slot).start()

        @pl.when(i == n_tiles - 1)
        def _():
            out_copy(i, slot).wait()


def _ffn(x2, mod5, norm_g4, wg, wu, wd, final_g, l, sub, final, nxt):
    rows, d = x2.shape
    f = wd.shape[0]
    tm, tf = FFN_TM, FFN_TF
    tps = SEQ // tm
    n_i, n_j = rows // tm, f // tf
    assert wg.shape == (n_j, d, tf)
    assert n_j >= 2, "the ring's hand-over runs at hidden tile 1"
    in_specs = [
        pl.BlockSpec(memory_space=pl.ANY),
        _mod_spec(l, sub, 0, tps), _mod_spec(l, sub, 1, tps), _mod_spec(l, sub, 2, tps),
        pl.BlockSpec((None, None, 1, d), lambda i, j: (l, sub, 0, 0)),
        pl.BlockSpec((None, d, tf), lambda i, j: (j, 0, 0)),
        pl.BlockSpec((None, d, tf), lambda i, j: (j, 0, 0)),
        pl.BlockSpec((tf, d), lambda i, j: (j, 0)),
        pl.BlockSpec((1, d), lambda i, j: (0, 0)),
    ]
    operands = [x2, mod5, mod5, mod5, norm_g4, wg, wu, wd, final_g]
    out_specs = [pl.BlockSpec(memory_space=pl.ANY)]
    out_shape = [jax.ShapeDtypeStruct((rows, d), F32)]
    if nxt is not None:
        ng, nu, nd, nl, ns = nxt
        gr, dr = d // n_i, f // (n_i * n_j)
        in_specs += [
            pl.BlockSpec((None, None, gr, tf), lambda i, j: (nl, ns, i, j)),
            pl.BlockSpec((None, None, gr, tf), lambda i, j: (nl, ns, i, j)),
            pl.BlockSpec((None, None, dr, d), lambda i, j: (nl, ns, i * n_j + j, 0)),
        ]
        operands += [ng, nu, nd]
        out_specs += [
            pl.BlockSpec((None, gr, tf), lambda i, j: (j, i, 0)),
            pl.BlockSpec((None, gr, tf), lambda i, j: (j, i, 0)),
            pl.BlockSpec((dr, d), lambda i, j: (i * n_j + j, 0)),
        ]
        out_shape += [jax.ShapeDtypeStruct((n_j, d, tf), BF16),
                      jax.ShapeDtypeStruct((n_j, d, tf), BF16),
                      jax.ShapeDtypeStruct((f, d), BF16)]
    outs = pl.pallas_call(
        functools.partial(_ffn_kernel, final=final, cast_next=nxt is not None),
        grid=(n_i, n_j),
        in_specs=in_specs,
        out_specs=out_specs,
        out_shape=out_shape,
        scratch_shapes=[
            pltpu.VMEM((2, tm, d), F32),
            pltpu.VMEM((tm, d), BF16),
            pltpu.VMEM((tm, d), F32),
            pltpu.SemaphoreType.DMA((2,)),
            pltpu.SemaphoreType.DMA((2,)),
        ],
        compiler_params=_cparams("arbitrary", "arbitrary"),
        name="ffn",
    )(*operands)
    return outs[0], tuple(outs[1:])


def _causal_conv(buf_ref, w_ref, out_ref, *, taps, halo, rows, width):
    base = halo - (taps - 1)
    win = CONV_ROWS + halo
    for c0 in range(0, width, LANES):
        cs = slice(c0, c0 + LANES)
        wts = w_ref[:, cs]
        for r0 in range(rows.start, rows.stop, CONV_ROWS):
            window = buf_ref[r0:r0 + win, cs]
            acc = None
            for r in range(SUBLANES):
                ks = [k for k in range(taps) if (base + k) % SUBLANES == r]
                if not ks:
                    continue
                shifted = window if r == 0 else pltpu.roll(window, win - r, axis=0)
                for k in ks:
                    q = (base + k) // SUBLANES * SUBLANES
                    term = wts[k:k + 1, :] * shifted[q:q + CONV_ROWS, :]
                    acc = term if acc is None else acc + term
            out_ref[r0:r0 + CONV_ROWS, cs] = acc


def _even_kernel(x_ref, shift_ref, scale_ref, gate_ref, g_ref, wa_ref, wr_ref, aconv_ref, lng_ref,
                 lnb_ref, bconv_ref, wout_ref, o_ref, h_ref, abuf, cbuf, bb_ref, ca_ref, cb_ref, *,
                 tiles_per_seq):
    tm = x_ref.shape[0]
    ng, _, cg = bb_ref.shape
    first = (pl.program_id(0) % tiles_per_seq) == 0

    @pl.when(first)
    def _():
        abuf[:, 0:A_HALO, :] = jnp.zeros((ng, A_HALO, cg), F32)
        cbuf[:, 0:B_HALO, :] = jnp.zeros((ng, B_HALO, cg), F32)

    @pl.when(jnp.logical_not(first))
    def _():
        abuf[:, 0:A_HALO, :] = abuf[:, tm:tm + A_HALO, :]
        cbuf[:, 0:B_HALO, :] = cbuf[:, tm:tm + B_HALO, :]

    x = x_ref[...]
    h_ref[...] = _norm_modulate(x, g_ref[...], shift_ref[...], scale_ref[...]).astype(BF16)

    h = h_ref[...]
    a = (jnp.dot(h, wa_ref[0], preferred_element_type=F32)
         * jax.nn.sigmoid(jnp.dot(h, wa_ref[1], preferred_element_type=F32)))
    for c in range(ng):
        abuf[c, A_HALO:A_HALO + tm, :] = a[:, c * cg:(c + 1) * cg]

    def group(c, carry):
        _causal_conv(abuf.at[c], aconv_ref.at[c], ca_ref.at[c], taps=A_KERNEL, halo=A_HALO,
                     rows=slice(0, tm), width=cg)
        piece = jnp.dot(h_ref[...], wr_ref[c], preferred_element_type=F32)
        bb_ref[c] = piece[:, 0:cg]
        cbuf[c, B_HALO:B_HALO + tm, :] = piece[:, cg:2 * cg] * piece[:, 2 * cg:3 * cg]
        _causal_conv(cbuf.at[c], bconv_ref.at[c], cb_ref.at[c], taps=B_KERNEL, halo=B_HALO,
                     rows=slice(0, tm), width=cg)
        return carry

    lax.fori_loop(0, ng, group, 0)

    conv = jnp.concatenate([ca_ref[c] for c in range(ng)], axis=1)
    mu = jnp.mean(conv, axis=-1, keepdims=True)
    ac = conv - mu
    var = jnp.mean(ac * ac, axis=-1, keepdims=True)
    a = _silu(ac * lax.rsqrt(var + EPS) * lng_ref[...] + lnb_ref[...]).astype(BF16)
    bo = jnp.concatenate([bb_ref[c] * cb_ref[c] for c in range(ng)], axis=1).astype(BF16)
    y = (jnp.dot(a, wout_ref[0:A_WIDTH, :], preferred_element_type=F32)
         + jnp.dot(bo, wout_ref[A_WIDTH:A_WIDTH + B_WIDTH, :], preferred_element_type=F32))
    o_ref[...] = x + (1.0 + gate_ref[...]) * y


def _gather_kernel(b_ref, c_ref, x_ref, o_ref):
    cg = b_ref.shape[-1]
    o_ref[:, 0:cg] = b_ref[...].astype(BF16)
    o_ref[:, cg:2 * cg] = c_ref[...].astype(BF16)
    o_ref[:, 2 * cg:3 * cg] = x_ref[...].astype(BF16)


def _group_rest(w_in, jx, cg):
    d = w_in.shape[1]
    ng = B_WIDTH // cg
    first = [(2 * A_WIDTH + k * B_WIDTH) // cg for k in range(3)]
    return pl.pallas_call(
        _gather_kernel,
        grid=(ng,),
        in_specs=[pl.BlockSpec((None, d, cg), lambda c, o=o: (jx, 0, o + c)) for o in first],
        out_specs=pl.BlockSpec((None, d, 3 * cg), lambda c: (c, 0, 0)),
        out_shape=jax.ShapeDtypeStruct((ng, d, 3 * cg), BF16),
        compiler_params=_cparams("arbitrary"),
        name="group_rest",
    )(w_in, w_in, w_in)


def _even_mix(x2, mod5, norm_g4, w_in, a_conv, ln_g, ln_b, b_conv, w_out, l, jx):
    rows, d = x2.shape
    tm, cg = EVEN_TM, EVEN_CG
    tps = SEQ // tm
    ng = A_WIDTH // cg
    assert A_WIDTH == B_WIDTH
    wa = _tile_major(w_in, (jx,), A_WIDTH, ntiles=2)
    wr = _group_rest(w_in, jx, cg)
    aconv = a_conv[jx].reshape(A_KERNEL, ng, cg).transpose(1, 0, 2)
    bconv = b_conv[jx].reshape(B_KERNEL, ng, cg).transpose(1, 0, 2)

    def whole(shape):
        return pl.BlockSpec(shape, lambda i: (0,) * len(shape))

    return pl.pallas_call(
        functools.partial(_even_kernel, tiles_per_seq=tps),
        grid=(rows // tm,),
        in_specs=[
            pl.BlockSpec((tm, d), lambda i: (i, 0)),
            _mod_spec(l, 1, 0, tps), _mod_spec(l, 1, 1, tps), _mod_spec(l, 1, 2, tps),
            pl.BlockSpec((None, None, 1, d), lambda i: (l, 1, 0, 0)),
            whole((2, d, A_WIDTH)),
            whole((ng, d, 3 * cg)),
            whole((ng, A_KERNEL, cg)),
            pl.BlockSpec((None, 1, A_WIDTH), lambda i: (jx, 0, 0)),
            pl.BlockSpec((None, 1, A_WIDTH), lambda i: (jx, 0, 0)),
            whole((ng, B_KERNEL, cg)),
            pl.BlockSpec((None, A_WIDTH + B_WIDTH, d), lambda i: (jx, 0, 0),
                         pipeline_mode=pl.Buffered(1)),
        ],
        out_specs=pl.BlockSpec((tm, d), lambda i: (i, 0)),
        out_shape=jax.ShapeDtypeStruct((rows, d), F32),
        scratch_shapes=[
            pltpu.VMEM((tm, d), BF16),
            pltpu.VMEM((ng, A_HALO + tm, cg), F32),
            pltpu.VMEM((ng, B_HALO + tm, cg), F32),
            pltpu.VMEM((ng, tm, cg), F32),
            pltpu.VMEM((ng, tm, cg), F32),
            pltpu.VMEM((ng, tm, cg), F32),
        ],
        compiler_params=_cparams("arbitrary"),
        name="even_mix",
    )(x2, mod5, mod5, mod5, norm_g4, wa, wr, aconv, ln_g, ln_b, bconv, w_out)


def _odd_kernel(x_ref, shift_ref, scale_ref, gate_ref, g_ref, win_ref, lng_ref, lnb_ref, ws_ref,
                bs_ref, wout_ref, o_ref, h_ref, v_ref, acc_ref, mu_ref, rstd_ref, *, nt):
    j = pl.program_id(1)
    tm = x_ref.shape[0]
    width = nt * win_ref.shape[1]
    edge_rows = _row_slices(tm, ODD_SPLIT)

    def in_proj(rs):
        return _gelu(jnp.dot(h_ref[rs, :], win_ref[...], preferred_element_type=F32))

    def row_stats(rs):
        total = jnp.zeros((rs.stop - rs.start, 1), F32)
        for t in range(nt):
            total = total + jnp.sum(v_ref[t, rs, :], axis=-1, keepdims=True)
        mu = total * (1.0 / width)
        sq = jnp.zeros_like(mu)
        for t in range(nt):
            vc = v_ref[t, rs, :] - mu
            sq = sq + jnp.sum(vc * vc, axis=-1, keepdims=True)
        mu_ref[rs, :] = mu
        rstd_ref[rs, :] = lax.rsqrt(sq * (1.0 / width) + EPS)

    @pl.when(j == 0)
    def _():
        for rs in edge_rows:
            h_ref[rs, :] = _norm_modulate(
                x_ref[rs, :], g_ref[...], shift_ref[...], scale_ref[...]).astype(BF16)
            v_ref[0, rs, :] = in_proj(rs)
            acc_ref[rs, :] = jnp.zeros((rs.stop - rs.start, acc_ref.shape[1]), F32)

    @pl.when(jnp.logical_and(j > 0, j < nt - 1))
    def _():
        v_ref[j] = in_proj(slice(0, tm))

    @pl.when(j == nt - 1)
    def _():
        for rs in edge_rows:
            v_ref[nt - 1, rs, :] = in_proj(rs)
            row_stats(rs)

    def gated_step(is_last):
        t = j - nt
        row = lax.broadcasted_iota(jnp.int32, (CHUNK, CHUNK), 0)
        col = lax.broadcasted_iota(jnp.int32, (CHUNK, CHUNK), 1)
        heads = ws_ref.shape[0]
        ws = [jnp.where(col <= row, ws_ref[hh], 0.0).astype(BF16) for hh in range(heads)]
        for rs in edge_rows:
            u = in_proj(rs)
            vn = ((v_ref[t, rs, :] - mu_ref[rs, :]) * rstd_ref[rs, :] * lng_ref[...]
                  + lnb_ref[...]).astype(BF16)
            gated = []
            for c0 in range(0, rs.stop - rs.start, CHUNK):
                cr = slice(c0, c0 + CHUNK)
                parts = []
                for hh in range(heads):
                    cs = slice(hh * C_HEAD_DIM, (hh + 1) * C_HEAD_DIM)
                    vs = jnp.dot(ws[hh], vn[cr, cs], preferred_element_type=F32) + bs_ref[hh]
                    parts.append((u[cr, cs] * vs).astype(BF16))
                gated.append(jnp.concatenate(parts, axis=1))
            gated = jnp.concatenate(gated, axis=0) if len(gated) > 1 else gated[0]
            y = jnp.dot(gated, wout_ref[...], preferred_element_type=F32)
            if is_last:
                o_ref[rs, :] = x_ref[rs, :] + (1.0 + gate_ref[...]) * (acc_ref[rs, :] + y)
            else:
                acc_ref[rs, :] += y

    @pl.when(jnp.logical_and(j >= nt, j < 2 * nt - 1))
    def _():
        gated_step(False)

    @pl.when(j == 2 * nt - 1)
    def _():
        gated_step(True)


def _odd_mix(x2, mod5, norm_g4, w_in, ln_g, ln_b, w_s, b_s4, w_out, l, jx):
    rows, d = x2.shape
    tm, tn = ODD_TM, ODD_TN
    tps = SEQ // tm
    nt = C_WIDTH // tn
    hpt = tn // C_HEAD_DIM

    def u_tile(j):
        return jnp.maximum(j - nt, 0)

    return pl.pallas_call(
        functools.partial(_odd_kernel, nt=nt),
        grid=(rows // tm, 2 * nt),
        in_specs=[
            pl.BlockSpec((tm, d), lambda i, j: (i, 0)),
            _mod_spec(l, 1, 0, tps), _mod_spec(l, 1, 1, tps), _mod_spec(l, 1, 2, tps),
            pl.BlockSpec((None, None, 1, d), lambda i, j: (l, 1, 0, 0)),
            pl.BlockSpec((None, d, tn), lambda i, j: (jnp.where(j < nt, j + nt, j - nt), 0, 0)),
            pl.BlockSpec((None, 1, tn), lambda i, j: (jx, 0, u_tile(j))),
            pl.BlockSpec((None, 1, tn), lambda i, j: (jx, 0, u_tile(j))),
            pl.BlockSpec((None, hpt, CHUNK, CHUNK), lambda i, j: (jx, u_tile(j), 0, 0)),
            pl.BlockSpec((None, hpt, CHUNK, 1), lambda i, j: (jx, u_tile(j), 0, 0)),
            pl.BlockSpec((None, tn, d), lambda i, j: (jx, u_tile(j), 0)),
        ],
        out_specs=pl.BlockSpec((tm, d), lambda i, j: (i, 0)),
        out_shape=jax.ShapeDtypeStruct((rows, d), F32),
        scratch_shapes=[
            pltpu.VMEM((tm, d), BF16),
            pltpu.VMEM((nt, tm, tn), F32),
            pltpu.VMEM((tm, d), F32),
            pltpu.VMEM((tm, 1), F32),
            pltpu.VMEM((tm, 1), F32),
        ],
        compiler_params=_cparams("arbitrary", "arbitrary"),
        name="odd_mix",
    )(x2, mod5, mod5, mod5, norm_g4, w_in, ln_g, ln_b, w_s, b_s4, w_out)


def kernel(x, c, ada_w, ada_b, norm_g, ffn_w_gate, ffn_w_up, ffn_w_down, ev_w_in, ev_a_conv,
           ev_a_ln_g, ev_a_ln_b, ev_b_conv, ev_w_out, od_w_in, od_v_ln_g, od_v_ln_b, od_w_s,
           od_b_s, od_w_out, final_g):
    bsz, seq, d = x.shape
    depth = ada_w.shape[0]
    assert (seq, d) == (SEQ, D_MODEL)

    mod5 = _modulation(c, ada_w, ada_b).reshape(depth, bsz, N_SUB * 3, 1, d)
    norm_g4 = norm_g.reshape(depth, N_SUB, 1, d)
    final_g2 = final_g.reshape(1, d)
    ffn_w = (ffn_w_gate, ffn_w_up, ffn_w_down)
    w_bf16 = (_tile_major(ffn_w_gate, (0, 0), FFN_TF), _tile_major(ffn_w_up, (0, 0), FFN_TF),
              ffn_w_down[0, 0].astype(BF16))
    ev_out = ev_w_out.astype(BF16)
    od_out = od_w_out.astype(BF16)
    ev_ln_g, ev_ln_b = (v.reshape(-1, 1, A_WIDTH) for v in (ev_a_ln_g, ev_a_ln_b))
    od_ln_g, od_ln_b = (v.reshape(-1, 1, C_WIDTH) for v in (od_v_ln_g, od_v_ln_b))
    od_bs = od_b_s.reshape(od_b_s.shape[0], -1, CHUNK, 1)

    x2 = x.reshape(bsz * seq, d)
    for l in range(depth):
        x2, w_bf16 = _ffn(x2, mod5, norm_g4, *w_bf16, final_g2, l, 0, False, ffn_w + (l, 1))
        jx = l // 2
        if l % 2 == 0:
            x2 = _even_mix(x2, mod5, norm_g4, ev_w_in, ev_a_conv, ev_ln_g, ev_ln_b, ev_b_conv,
                           ev_out, l, jx)
        else:
            od_in = _tile_major(od_w_in, (jx,), ODD_TN)
            x2 = _odd_mix(x2, mod5, norm_g4, od_in, od_ln_g, od_ln_b, od_w_s, od_bs, od_out,
                          l, jx)
        is_last = l == depth - 1
        x2, w_bf16 = _ffn(x2, mod5, norm_g4, *w_bf16, final_g2, l, 2, is_last,
                          None if is_last else ffn_w + (l + 1, 0))
    return x2.reshape(bsz, seq, d)
```

```python
import functools

import jax
import jax.numpy as jnp
from jax import lax
from jax.experimental import pallas as pl
from jax.experimental.pallas import tpu as pltpu

D_MODEL = 2048
SEQ = 2048
FFN_HIDDEN = 5632
A_WIDTH = 1024
B_WIDTH = 1024
A_KERNEL = 31
B_KERNEL = 3
C_WIDTH = 4096
C_HEAD_DIM = 256
CHUNK = 128
N_SUB = 3
EPS = 1e-6

F32 = jnp.float32
BF16 = jnp.bfloat16

SUBLANES = 8
LANES = 128
A_HALO = 32
B_HALO = 8
CONV_ROWS = 32
VMEM_LIMIT = 56 * 1024 * 1024

FFN_TM = 1024
FFN_TF = 512
FFN_SPLIT = 2
EDGE_SPLIT = 2
EVEN_TM = 256
EVEN_CG = 512
ODD_TM = 512
ODD_TN = 1024
ODD_SPLIT = 2
MOD_TN = 1024


def _cparams(*sem):
    return pltpu.CompilerParams(dimension_semantics=sem, vmem_limit_bytes=VMEM_LIMIT)


def _silu(v):
    return v * jax.nn.sigmoid(v)


def _gelu(v):
    return 0.5 * v * (1.0 + lax.erf(v * (0.5 ** 0.5)))


def _norm_modulate(x, g, shift, scale):
    ms = jnp.mean(x * x, axis=-1, keepdims=True)
    y = x * lax.rsqrt(ms + EPS) * g
    return y * (1.0 + scale) + shift


def _cast_kernel(*refs):
    *w_refs, o_ref = refs
    tn = w_refs[0].shape[-1]
    for k, w_ref in enumerate(w_refs):
        o_ref[:, k * tn:(k + 1) * tn] = w_ref[...].astype(BF16)


def _tile_major(ws, lead, tn, ntiles=None, first=None):
    d, n = ws[0].shape[-2:]
    ntiles = n // tn if ntiles is None else ntiles
    first = [0] * len(ws) if first is None else first
    return pl.pallas_call(
        _cast_kernel,
        grid=(ntiles,),
        in_specs=[pl.BlockSpec((None,) * len(lead) + (d, tn), lambda t, o=o: (*lead, 0, o + t))
                  for o in first],
        out_specs=pl.BlockSpec((None, d, len(ws) * tn), lambda t: (t, 0, 0)),
        out_shape=jax.ShapeDtypeStruct((ntiles, d, len(ws) * tn), BF16),
        compiler_params=_cparams("arbitrary"),
        name="cast_tiles",
    )(*ws)


def _row_slices(rows, parts):
    sub = rows // parts
    return [slice(r * sub, (r + 1) * sub) for r in range(parts)]


def _mod_kernel(c_ref, w_ref, b_ref, o_ref):
    cond = _silu(c_ref[...]).astype(BF16)
    o_ref[...] = jnp.dot(cond, w_ref[...].astype(BF16), preferred_element_type=F32) + b_ref[...]


def _modulation(c, ada_w, ada_b):
    depth, d, n = ada_w.shape
    bsz = c.shape[0]
    tn = MOD_TN
    return pl.pallas_call(
        _mod_kernel,
        grid=(depth, n // tn),
        in_specs=[
            pl.BlockSpec((bsz, d), lambda l, j: (0, 0)),
            pl.BlockSpec((None, d, tn), lambda l, j: (l, 0, j)),
            pl.BlockSpec((None, 1, tn), lambda l, j: (l, 0, j)),
        ],
        out_specs=pl.BlockSpec((None, bsz, tn), lambda l, j: (l, 0, j)),
        out_shape=jax.ShapeDtypeStruct((depth, bsz, n), F32),
        compiler_params=_cparams("arbitrary", "arbitrary"),
        name="modulation",
    )(c, ada_w, ada_b.reshape(depth, 1, n))


def _mod_spec(l, sub, kind, rows_per_batch_tile):
    k = sub * 3 + kind
    return pl.BlockSpec((None, None, None, 1, D_MODEL),
                        lambda i, *_: (l, i // rows_per_batch_tile, k, 0, 0))


def _ffn_kernel(*refs, final, cast_next):
    (x_hbm, shift_ref, scale_ref, gate_ref, g_ref, wgu_ref, wd_ref, fg_ref) = refs[:8]
    refs = refs[8:]
    tf = wd_ref.shape[0]
    if cast_next:
        (ng_ref, nu_ref, nd_ref, o_hbm, cgu_ref, cd_ref) = refs[:6]
        refs = refs[6:]
    else:
        o_hbm, refs = refs[0], refs[1:]
    xbuf, h_ref, acc_ref, in_sem, out_sem = refs

    i = pl.program_id(0)
    j = pl.program_id(1)
    n_tiles = pl.num_programs(0)
    last = pl.num_programs(1) - 1
    tm = xbuf.shape[1]
    slot = i % 2

    def in_copy(tile, s):
        return pltpu.make_async_copy(
            x_hbm.at[pl.ds(tile * tm, tm), :], xbuf.at[s], in_sem.at[s])

    def out_copy(tile, s):
        return pltpu.make_async_copy(
            xbuf.at[s], o_hbm.at[pl.ds(tile * tm, tm), :], out_sem.at[s])

    @pl.when(jnp.logical_and(i == 0, j == 0))
    def _():
        in_copy(0, 0).start()

    @pl.when(j == 0)
    def _():
        in_copy(i, slot).wait()

    @pl.when(j == 1)
    def _():
        @pl.when(i >= 1)
        def _():
            out_copy(i - 1, 1 - slot).wait()

        @pl.when(i + 1 < n_tiles)
        def _():
            in_copy(i + 1, 1 - slot).start()

    if cast_next:
        cgu_ref[:, 0:tf] = ng_ref[...].astype(BF16)
        cgu_ref[:, tf:2 * tf] = nu_ref[...].astype(BF16)
        cd_ref[...] = nd_ref[...].astype(BF16)

    def step(rs, is_first, is_last):
        if is_first:
            h_ref[rs, :] = _norm_modulate(
                xbuf[slot, rs, :], g_ref[...], shift_ref[...], scale_ref[...]).astype(BF16)
        h = h_ref[rs, :]
        gate_up = jnp.dot(h, wgu_ref[...], preferred_element_type=F32)
        a = (_silu(gate_up[:, 0:tf]) * gate_up[:, tf:2 * tf]).astype(BF16)
        down = jnp.dot(a, wd_ref[...], preferred_element_type=F32)
        if is_first:
            acc_ref[rs, :] = down
        elif is_last:
            y = xbuf[slot, rs, :] + (0.5 * (1.0 + gate_ref[...])) * (acc_ref[rs, :] + down)
            if final:
                ms = jnp.mean(y * y, axis=-1, keepdims=True)
                y = y * lax.rsqrt(ms + EPS) * fg_ref[...]
            xbuf[slot, rs, :] = y
        else:
            acc_ref[rs, :] += down

    edge_rows = _row_slices(tm, FFN_SPLIT * EDGE_SPLIT)

    @pl.when(j == 0)
    def _():
        for rs in edge_rows:
            step(rs, True, False)

    @pl.when(jnp.logical_and(j > 0, j < last))
    def _():
        for rs in _row_slices(tm, FFN_SPLIT):
            step(rs, False, False)

    @pl.when(j == last)
    def _():
        for rs in edge_rows:
            step(rs, False, True)
        out_copy(i, slot).start()

        @pl.when(i == n_tiles - 1)
        def _():
            out_copy(i, slot).wait()


def _ffn(x2, mod5, norm_g4, wgu, wd, final_g, l, sub, final, nxt):
    rows, d = x2.shape
    f = wd.shape[0]
    tm, tf = FFN_TM, FFN_TF
    tps = SEQ // tm
    n_i, n_j = rows // tm, f // tf
    assert wgu.shape == (n_j, d, 2 * tf)
    assert n_j >= 2, "the ring's hand-over runs at hidden tile 1"
    in_specs = [
        pl.BlockSpec(memory_space=pl.ANY),
        _mod_spec(l, sub, 0, tps), _mod_spec(l, sub, 1, tps), _mod_spec(l, sub, 2, tps),
        pl.BlockSpec((None, None, 1, d), lambda i, j: (l, sub, 0, 0)),
        pl.BlockSpec((None, d, 2 * tf), lambda i, j: (j, 0, 0)),
        pl.BlockSpec((tf, d), lambda i, j: (j, 0)),
        pl.BlockSpec((1, d), lambda i, j: (0, 0)),
    ]
    operands = [x2, mod5, mod5, mod5, norm_g4, wgu, wd, final_g]
    out_specs = [pl.BlockSpec(memory_space=pl.ANY)]
    out_shape = [jax.ShapeDtypeStruct((rows, d), F32)]
    if nxt is not None:
        ng, nu, nd, nl, ns = nxt
        gr, dr = d // n_i, f // (n_i * n_j)
        in_specs += [
            pl.BlockSpec((None, None, gr, tf), lambda i, j: (nl, ns, i, j)),
            pl.BlockSpec((None, None, gr, tf), lambda i, j: (nl, ns, i, j)),
            pl.BlockSpec((None, None, dr, d), lambda i, j: (nl, ns, i * n_j + j, 0)),
        ]
        operands += [ng, nu, nd]
        out_specs += [
            pl.BlockSpec((None, gr, 2 * tf), lambda i, j: (j, i, 0)),
            pl.BlockSpec((dr, d), lambda i, j: (i * n_j + j, 0)),
        ]
        out_shape += [jax.ShapeDtypeStruct((n_j, d, 2 * tf), BF16),
                      jax.ShapeDtypeStruct((f, d), BF16)]
    outs = pl.pallas_call(
        functools.partial(_ffn_kernel, final=final, cast_next=nxt is not None),
        grid=(n_i, n_j),
        in_specs=in_specs,
        out_specs=out_specs,
        out_shape=out_shape,
        scratch_shapes=[
            pltpu.VMEM((2, tm, d), F32),
            pltpu.VMEM((tm, d), BF16),
            pltpu.VMEM((tm, d), F32),
            pltpu.SemaphoreType.DMA((2,)),
            pltpu.SemaphoreType.DMA((2,)),
        ],
        compiler_params=_cparams("arbitrary", "arbitrary"),
        name="ffn",
    )(*operands)
    return outs[0], tuple(outs[1:])


def _causal_conv(buf_ref, w_ref, out_ref, *, taps, halo, rows, width):
    base = halo - (taps - 1)
    win = CONV_ROWS + halo
    for c0 in range(0, width, LANES):
        cs = slice(c0, c0 + LANES)
        wts = w_ref[:, cs]
        for r0 in range(rows.start, rows.stop, CONV_ROWS):
            window = buf_ref[r0:r0 + win, cs]
            acc = None
            for r in range(SUBLANES):
                ks = [k for k in range(taps) if (base + k) % SUBLANES == r]
                if not ks:
                    continue
                shifted = window if r == 0 else pltpu.roll(window, win - r, axis=0)
                for k in ks:
                    q = (base + k) // SUBLANES * SUBLANES
                    term = wts[k:k + 1, :] * shifted[q:q + CONV_ROWS, :]
                    acc = term if acc is None else acc + term
            out_ref[r0:r0 + CONV_ROWS, cs] = acc


def _even_kernel(x_ref, shift_ref, scale_ref, gate_ref, g_ref, wa_ref, wr_ref, aconv_ref, lng_ref,
                 lnb_ref, bconv_ref, wout_ref, o_ref, h_ref, abuf, cbuf, bb_ref, ca_ref, cb_ref, *,
                 tiles_per_seq):
    tm = x_ref.shape[0]
    ng, _, cg = bb_ref.shape
    first = (pl.program_id(0) % tiles_per_seq) == 0

    @pl.when(first)
    def _():
        abuf[:, 0:A_HALO, :] = jnp.zeros((ng, A_HALO, cg), F32)
        cbuf[:, 0:B_HALO, :] = jnp.zeros((ng, B_HALO, cg), F32)

    @pl.when(jnp.logical_not(first))
    def _():
        abuf[:, 0:A_HALO, :] = abuf[:, tm:tm + A_HALO, :]
        cbuf[:, 0:B_HALO, :] = cbuf[:, tm:tm + B_HALO, :]

    x = x_ref[...]
    h_ref[...] = _norm_modulate(x, g_ref[...], shift_ref[...], scale_ref[...]).astype(BF16)

    h = h_ref[...]
    a = (jnp.dot(h, wa_ref[0], preferred_element_type=F32)
         * jax.nn.sigmoid(jnp.dot(h, wa_ref[1], preferred_element_type=F32)))
    for c in range(ng):
        abuf[c, A_HALO:A_HALO + tm, :] = a[:, c * cg:(c + 1) * cg]

    def group(c, carry):
        _causal_conv(abuf.at[c], aconv_ref.at[c], ca_ref.at[c], taps=A_KERNEL, halo=A_HALO,
                     rows=slice(0, tm), width=cg)
        piece = jnp.dot(h_ref[...], wr_ref[c], preferred_element_type=F32)
        bb_ref[c] = piece[:, 0:cg]
        cbuf[c, B_HALO:B_HALO + tm, :] = piece[:, cg:2 * cg] * piece[:, 2 * cg:3 * cg]
        _causal_conv(cbuf.at[c], bconv_ref.at[c], cb_ref.at[c], taps=B_KERNEL, halo=B_HALO,
                     rows=slice(0, tm), width=cg)
        return carry

    lax.fori_loop(0, ng, group, 0)

    conv = jnp.concatenate([ca_ref[c] for c in range(ng)], axis=1)
    mu = jnp.mean(conv, axis=-1, keepdims=True)
    ac = conv - mu
    var = jnp.mean(ac * ac, axis=-1, keepdims=True)
    a = _silu(ac * lax.rsqrt(var + EPS) * lng_ref[...] + lnb_ref[...]).astype(BF16)
    bo = jnp.concatenate([bb_ref[c] * cb_ref[c] for c in range(ng)], axis=1).astype(BF16)
    y = (jnp.dot(a, wout_ref[0:A_WIDTH, :], preferred_element_type=F32)
         + jnp.dot(bo, wout_ref[A_WIDTH:A_WIDTH + B_WIDTH, :], preferred_element_type=F32))
    o_ref[...] = x + (1.0 + gate_ref[...]) * y


def _even_mix(x2, mod5, norm_g4, w_in, a_conv, ln_g, ln_b, b_conv, w_out, l, jx):
    rows, d = x2.shape
    tm, cg = EVEN_TM, EVEN_CG
    tps = SEQ // tm
    ng = A_WIDTH // cg
    assert A_WIDTH == B_WIDTH
    wa = _tile_major([w_in], (jx,), A_WIDTH, ntiles=2)
    rest_first = [(2 * A_WIDTH + k * B_WIDTH) // cg for k in range(3)]
    wr = _tile_major([w_in] * 3, (jx,), cg, ntiles=ng, first=rest_first)
    aconv = a_conv[jx].reshape(A_KERNEL, ng, cg).transpose(1, 0, 2)
    bconv = b_conv[jx].reshape(B_KERNEL, ng, cg).transpose(1, 0, 2)

    def whole(shape):
        return pl.BlockSpec(shape, lambda i: (0,) * len(shape))

    return pl.pallas_call(
        functools.partial(_even_kernel, tiles_per_seq=tps),
        grid=(rows // tm,),
        in_specs=[
            pl.BlockSpec((tm, d), lambda i: (i, 0)),
            _mod_spec(l, 1, 0, tps), _mod_spec(l, 1, 1, tps), _mod_spec(l, 1, 2, tps),
            pl.BlockSpec((None, None, 1, d), lambda i: (l, 1, 0, 0)),
            whole((2, d, A_WIDTH)),
            whole((ng, d, 3 * cg)),
            whole((ng, A_KERNEL, cg)),
            pl.BlockSpec((None, 1, A_WIDTH), lambda i: (jx, 0, 0)),
            pl.BlockSpec((None, 1, A_WIDTH), lambda i: (jx, 0, 0)),
            whole((ng, B_KERNEL, cg)),
            pl.BlockSpec((None, A_WIDTH + B_WIDTH, d), lambda i: (jx, 0, 0),
                         pipeline_mode=pl.Buffered(1)),
        ],
        out_specs=pl.BlockSpec((tm, d), lambda i: (i, 0)),
        out_shape=jax.ShapeDtypeStruct((rows, d), F32),
        scratch_shapes=[
            pltpu.VMEM((tm, d), BF16),
            pltpu.VMEM((ng, A_HALO + tm, cg), F32),
            pltpu.VMEM((ng, B_HALO + tm, cg), F32),
            pltpu.VMEM((ng, tm, cg), F32),
            pltpu.VMEM((ng, tm, cg), F32),
            pltpu.VMEM((ng, tm, cg), F32),
        ],
        compiler_params=_cparams("arbitrary"),
        name="even_mix",
    )(x2, mod5, mod5, mod5, norm_g4, wa, wr, aconv, ln_g, ln_b, bconv, w_out)


def _odd_kernel(x_ref, shift_ref, scale_ref, gate_ref, g_ref, win_ref, lng_ref, lnb_ref, ws_ref,
                bs_ref, wout_ref, o_ref, h_ref, v_ref, acc_ref, mu_ref, rstd_ref, *, nt):
    j = pl.program_id(1)
    tm = x_ref.shape[0]
    width = nt * win_ref.shape[1]
    edge_rows = _row_slices(tm, ODD_SPLIT)

    def in_proj(rs):
        return _gelu(jnp.dot(h_ref[rs, :], win_ref[...], preferred_element_type=F32))

    def row_stats(rs):
        total = jnp.zeros((rs.stop - rs.start, 1), F32)
        for t in range(nt):
            total = total + jnp.sum(v_ref[t, rs, :], axis=-1, keepdims=True)
        mu = total * (1.0 / width)
        sq = jnp.zeros_like(mu)
        for t in range(nt):
            vc = v_ref[t, rs, :] - mu
            sq = sq + jnp.sum(vc * vc, axis=-1, keepdims=True)
        mu_ref[rs, :] = mu
        rstd_ref[rs, :] = lax.rsqrt(sq * (1.0 / width) + EPS)

    @pl.when(j == 0)
    def _():
        for rs in edge_rows:
            h_ref[rs, :] = _norm_modulate(
                x_ref[rs, :], g_ref[...], shift_ref[...], scale_ref[...]).astype(BF16)
            v_ref[0, rs, :] = in_proj(rs)
            acc_ref[rs, :] = jnp.zeros((rs.stop - rs.start, acc_ref.shape[1]), F32)

    @pl.when(jnp.logical_and(j > 0, j < nt - 1))
    def _():
        v_ref[j] = in_proj(slice(0, tm))

    @pl.when(j == nt - 1)
    def _():
        for rs in edge_rows:
            v_ref[nt - 1, rs, :] = in_proj(rs)
            row_stats(rs)

    def gated_step(is_last):
        t = j - nt
        row = lax.broadcasted_iota(jnp.int32, (CHUNK, CHUNK), 0)
        col = lax.broadcasted_iota(jnp.int32, (CHUNK, CHUNK), 1)
        heads = ws_ref.shape[0]
        ws = [jnp.where(col <= row, ws_ref[hh], 0.0).astype(BF16) for hh in range(heads)]
        for rs in edge_rows:
            vn = ((v_ref[t, rs, :] - mu_ref[rs, :]) * rstd_ref[rs, :] * lng_ref[...]
                  + lnb_ref[...]).astype(BF16)
            u = in_proj(rs)
            gated = []
            for c0 in range(0, rs.stop - rs.start, CHUNK):
                cr = slice(c0, c0 + CHUNK)
                parts = []
                for hh in range(heads):
                    cs = slice(hh * C_HEAD_DIM, (hh + 1) * C_HEAD_DIM)
                    vs = jnp.dot(ws[hh], vn[cr, cs], preferred_element_type=F32) + bs_ref[hh]
                    parts.append((u[cr, cs] * vs).astype(BF16))
                gated.append(jnp.concatenate(parts, axis=1))
            gated = jnp.concatenate(gated, axis=0) if len(gated) > 1 else gated[0]
            y = jnp.dot(gated, wout_ref[...], preferred_element_type=F32)
            if is_last:
                o_ref[rs, :] = x_ref[rs, :] + (1.0 + gate_ref[...]) * (acc_ref[rs, :] + y)
            else:
                acc_ref[rs, :] += y

    @pl.when(jnp.logical_and(j >= nt, j < 2 * nt - 1))
    def _():
        gated_step(False)

    @pl.when(j == 2 * nt - 1)
    def _():
        gated_step(True)


def _odd_mix(x2, mod5, norm_g4, w_in, ln_g, ln_b, w_s, b_s4, w_out, l, jx):
    rows, d = x2.shape
    tm, tn = ODD_TM, ODD_TN
    tps = SEQ // tm
    nt = C_WIDTH // tn
    hpt = tn // C_HEAD_DIM

    def u_tile(j):
        return jnp.maximum(j - nt, 0)

    return pl.pallas_call(
        functools.partial(_odd_kernel, nt=nt),
        grid=(rows // tm, 2 * nt),
        in_specs=[
            pl.BlockSpec((tm, d), lambda i, j: (i, 0)),
            _mod_spec(l, 1, 0, tps), _mod_spec(l, 1, 1, tps), _mod_spec(l, 1, 2, tps),
            pl.BlockSpec((None, None, 1, d), lambda i, j: (l, 1, 0, 0)),
            pl.BlockSpec((None, d, tn), lambda i, j: (jnp.where(j < nt, j + nt, j - nt), 0, 0)),
            pl.BlockSpec((None, 1, tn), lambda i, j: (jx, 0, u_tile(j))),
            pl.BlockSpec((None, 1, tn), lambda i, j: (jx, 0, u_tile(j))),
            pl.BlockSpec((None, hpt, CHUNK, CHUNK), lambda i, j: (jx, u_tile(j), 0, 0)),
            pl.BlockSpec((None, hpt, CHUNK, 1), lambda i, j: (jx, u_tile(j), 0, 0)),
            pl.BlockSpec((None, tn, d), lambda i, j: (jx, u_tile(j), 0)),
        ],
        out_specs=pl.BlockSpec((tm, d), lambda i, j: (i, 0)),
        out_shape=jax.ShapeDtypeStruct((rows, d), F32),
        scratch_shapes=[
            pltpu.VMEM((tm, d), BF16),
            pltpu.VMEM((nt, tm, tn), F32),
            pltpu.VMEM((tm, d), F32),
            pltpu.VMEM((tm, 1), F32),
            pltpu.VMEM((tm, 1), F32),
        ],
        compiler_params=_cparams("arbitrary", "arbitrary"),
        name="odd_mix",
    )(x2, mod5, mod5, mod5, norm_g4, w_in, ln_g, ln_b, w_s, b_s4, w_out)


def kernel(x, c, ada_w, ada_b, norm_g, ffn_w_gate, ffn_w_up, ffn_w_down, ev_w_in, ev_a_conv,
           ev_a_ln_g, ev_a_ln_b, ev_b_conv, ev_w_out, od_w_in, od_v_ln_g, od_v_ln_b, od_w_s,
           od_b_s, od_w_out, final_g):
    bsz, seq, d = x.shape
    depth = ada_w.shape[0]
    assert (seq, d) == (SEQ, D_MODEL)

    mod5 = _modulation(c, ada_w, ada_b).reshape(depth, bsz, N_SUB * 3, 1, d)
    norm_g4 = norm_g.reshape(depth, N_SUB, 1, d)
    final_g2 = final_g.reshape(1, d)
    ffn_w = (ffn_w_gate, ffn_w_up, ffn_w_down)
    w_bf16 = (_tile_major([ffn_w_gate, ffn_w_up], (0, 0), FFN_TF), ffn_w_down[0, 0].astype(BF16))
    ev_out = ev_w_out.astype(BF16)
    od_out = od_w_out.astype(BF16)
    ev_ln_g, ev_ln_b = (v.reshape(-1, 1, A_WIDTH) for v in (ev_a_ln_g, ev_a_ln_b))
    od_ln_g, od_ln_b = (v.reshape(-1, 1, C_WIDTH) for v in (od_v_ln_g, od_v_ln_b))
    od_bs = od_b_s.reshape(od_b_s.shape[0], -1, CHUNK, 1)

    x2 = x.reshape(bsz * seq, d)
    for l in range(depth):
        x2, w_bf16 = _ffn(x2, mod5, norm_g4, *w_bf16, final_g2, l, 0, False, ffn_w + (l, 1))
        jx = l // 2
        if l % 2 == 0:
            x2 = _even_mix(x2, mod5, norm_g4, ev_w_in, ev_a_conv, ev_ln_g, ev_ln_b, ev_b_conv,
                           ev_out, l, jx)
        else:
            od_in = _tile_major([od_w_in], (jx,), ODD_TN)
            x2 = _odd_mix(x2, mod5, norm_g4, od_in, od_ln_g, od_ln_b, od_w_s, od_bs, od_out,
                          l, jx)
        is_last = l == depth - 1
        x2, w_bf16 = _ffn(x2, mod5, norm_g4, *w_bf16, final_g2, l, 2, is_last,
                          None if is_last else ffn_w + (l + 1, 0))
    return x2.reshape(bsz, seq, d)
```

```python
import functools

import jax
import jax.numpy as jnp
from jax import lax
from jax.experimental import pallas as pl
from jax.experimental.pallas import tpu as pltpu

D_MODEL = 2048
SEQ = 2048
FFN_HIDDEN = 5632
A_WIDTH = 1024
B_WIDTH = 1024
A_KERNEL = 31
B_KERNEL = 3
C_WIDTH = 4096
C_HEAD_DIM = 256
CHUNK = 128
N_SUB = 3
EPS = 1e-6

F32 = jnp.float32
BF16 = jnp.bfloat16

SUBLANES = 8
LANES = 128
A_HALO = 32
B_HALO = 8
CONV_ROWS = 32
VMEM_LIMIT = 56 * 1024 * 1024

FFN_TM = 1024
FFN_TF = 512
FFN_SPLIT = 2
EDGE_SPLIT = 2
EVEN_TM = 256
EVEN_CG = 512
ODD_TM = 512
ODD_TN = 1024
ODD_SPLIT = 2
MOD_TN = 1024


def _cparams(*sem):
    return pltpu.CompilerParams(dimension_semantics=sem, vmem_limit_bytes=VMEM_LIMIT)


def _silu(v):
    return v * jax.nn.sigmoid(v)


def _gelu(v):
    return 0.5 * v * (1.0 + lax.erf(v * (0.5 ** 0.5)))


def _norm_modulate(x, g, shift, scale):
    ms = jnp.mean(x * x, axis=-1, keepdims=True)
    y = x * lax.rsqrt(ms + EPS) * g
    return y * (1.0 + scale) + shift


def _cast_kernel(*refs):
    *w_refs, o_ref = refs
    tn = w_refs[0].shape[-1]
    for k, w_ref in enumerate(w_refs):
        o_ref[:, k * tn:(k + 1) * tn] = w_ref[...].astype(BF16)


def _tile_major(ws, lead, tn, ntiles=None, first=None):
    d, n = ws[0].shape[-2:]
    ntiles = n // tn if ntiles is None else ntiles
    first = [0] * len(ws) if first is None else first
    return pl.pallas_call(
        _cast_kernel,
        grid=(ntiles,),
        in_specs=[pl.BlockSpec((None,) * len(lead) + (d, tn), lambda t, o=o: (*lead, 0, o + t))
                  for o in first],
        out_specs=pl.BlockSpec((None, d, len(ws) * tn), lambda t: (t, 0, 0)),
        out_shape=jax.ShapeDtypeStruct((ntiles, d, len(ws) * tn), BF16),
        compiler_params=_cparams("arbitrary"),
        name="cast_tiles",
    )(*ws)


def _row_slices(rows, parts):
    sub = rows // parts
    return [slice(r * sub, (r + 1) * sub) for r in range(parts)]


def _mod_kernel(c_ref, w_ref, b_ref, o_ref):
    cond = _silu(c_ref[...]).astype(BF16)
    o_ref[...] = jnp.dot(cond, w_ref[...].astype(BF16), preferred_element_type=F32) + b_ref[...]


def _modulation(c, ada_w, ada_b):
    depth, d, n = ada_w.shape
    bsz = c.shape[0]
    tn = MOD_TN
    return pl.pallas_call(
        _mod_kernel,
        grid=(depth, n // tn),
        in_specs=[
            pl.BlockSpec((bsz, d), lambda l, j: (0, 0)),
            pl.BlockSpec((None, d, tn), lambda l, j: (l, 0, j)),
            pl.BlockSpec((None, 1, tn), lambda l, j: (l, 0, j)),
        ],
        out_specs=pl.BlockSpec((None, bsz, tn), lambda l, j: (l, 0, j)),
        out_shape=jax.ShapeDtypeStruct((depth, bsz, n), F32),
        compiler_params=_cparams("arbitrary", "arbitrary"),
        name="modulation",
    )(c, ada_w, ada_b.reshape(depth, 1, n))


def _mod_spec(l, sub, kind, rows_per_batch_tile):
    k = sub * 3 + kind
    return pl.BlockSpec((None, None, None, 1, D_MODEL),
                        lambda i, *_: (l, i // rows_per_batch_tile, k, 0, 0))


def _ffn_kernel(*refs, final, cast_next):
    (x_hbm, shift_ref, scale_ref, gate_ref, g_ref, wg_ref, wu_ref, wd_ref, fg_ref) = refs[:9]
    refs = refs[9:]
    if cast_next:
        (ng_ref, nu_ref, nd_ref, o_hbm, cg_ref, cu_ref, cd_ref) = refs[:7]
        refs = refs[7:]
    else:
        o_hbm, refs = refs[0], refs[1:]
    xbuf, h_ref, acc_ref, in_sem, out_sem = refs

    i = pl.program_id(0)
    j = pl.program_id(1)
    n_tiles = pl.num_programs(0)
    last = pl.num_programs(1) - 1
    tm = xbuf.shape[1]
    slot = i % 2

    def in_copy(tile, s):
        return pltpu.make_async_copy(
            x_hbm.at[pl.ds(tile * tm, tm), :], xbuf.at[s], in_sem.at[s])

    def out_copy(tile, s):
        return pltpu.make_async_copy(
            xbuf.at[s], o_hbm.at[pl.ds(tile * tm, tm), :], out_sem.at[s])

    @pl.when(jnp.logical_and(i == 0, j == 0))
    def _():
        in_copy(0, 0).start()

    @pl.when(j == 0)
    def _():
        in_copy(i, slot).wait()

    @pl.when(j == 1)
    def _():
        @pl.when(i >= 1)
        def _():
            out_copy(i - 1, 1 - slot).wait()

        @pl.when(i + 1 < n_tiles)
        def _():
            in_copy(i + 1, 1 - slot).start()

    if cast_next:
        cg_ref[...] = ng_ref[...].astype(BF16)
        cu_ref[...] = nu_ref[...].astype(BF16)
        cd_ref[...] = nd_ref[...].astype(BF16)

    def step(rs, is_first, is_last):
        if is_first:
            h_ref[rs, :] = _norm_modulate(
                xbuf[slot, rs, :], g_ref[...], shift_ref[...], scale_ref[...]).astype(BF16)
        h = h_ref[rs, :]
        gate = jnp.dot(h, wg_ref[...], preferred_element_type=F32)
        up = jnp.dot(h, wu_ref[...], preferred_element_type=F32)
        a = (_silu(gate) * up).astype(BF16)
        down = jnp.dot(a, wd_ref[...], preferred_element_type=F32)
        if is_first:
            acc_ref[rs, :] = down
        elif is_last:
            y = xbuf[slot, rs, :] + (0.5 * (1.0 + gate_ref[...])) * (acc_ref[rs, :] + down)
            if final:
                ms = jnp.mean(y * y, axis=-1, keepdims=True)
                y = y * lax.rsqrt(ms + EPS) * fg_ref[...]
            xbuf[slot, rs, :] = y
        else:
            acc_ref[rs, :] += down

    edge_rows = _row_slices(tm, FFN_SPLIT * EDGE_SPLIT)

    @pl.when(j == 0)
    def _():
        for rs in edge_rows:
            step(rs, True, False)

    @pl.when(jnp.logical_and(j > 0, j < last))
    def _():
        for rs in _row_slices(tm, FFN_SPLIT):
            step(rs, False, False)

    @pl.when(j == last)
    def _():
        for rs in edge_rows:
            step(rs, False, True)
        out_copy(i, slot).start()

        @pl.when(i == n_tiles - 1)
        def _():
            out_copy(i, slot).wait()


def _ffn(x2, mod5, norm_g4, wg, wu, wd, final_g, l, sub, final, nxt):
    rows, d = x2.shape
    f = wd.shape[0]
    tm, tf = FFN_TM, FFN_TF
    tps = SEQ // tm
    n_i, n_j = rows // tm, f // tf
    assert wg.shape == (n_j, d, tf)
    assert n_j >= 2, "the ring's hand-over runs at hidden tile 1"
    in_specs = [
        pl.BlockSpec(memory_space=pl.ANY),
        _mod_spec(l, sub, 0, tps), _mod_spec(l, sub, 1, tps), _mod_spec(l, sub, 2, tps),
        pl.BlockSpec((None, None, 1, d), lambda i, j: (l, sub, 0, 0)),
        pl.BlockSpec((None, d, tf), lambda i, j: (j, 0, 0)),
        pl.BlockSpec((None, d, tf), lambda i, j: (j, 0, 0)),
        pl.BlockSpec((tf, d), lambda i, j: (j, 0)),
        pl.BlockSpec((1, d), lambda i, j: (0, 0)),
    ]
    operands = [x2, mod5, mod5, mod5, norm_g4, wg, wu, wd, final_g]
    out_specs = [pl.BlockSpec(memory_space=pl.ANY)]
    out_shape = [jax.ShapeDtypeStruct((rows, d), F32)]
    if nxt is not None:
        ng, nu, nd, nl, ns = nxt
        gr, dr = d // n_i, f // (n_i * n_j)
        in_specs += [
            pl.BlockSpec((None, None, gr, tf), lambda i, j: (nl, ns, i, j)),
            pl.BlockSpec((None, None, gr, tf), lambda i, j: (nl, ns, i, j)),
            pl.BlockSpec((None, None, dr, d), lambda i, j: (nl, ns, i * n_j + j, 0)),
        ]
        operands += [ng, nu, nd]
        out_specs += [
            pl.BlockSpec((None, gr, tf), lambda i, j: (j, i, 0)),
            pl.BlockSpec((None, gr, tf), lambda i, j: (j, i, 0)),
            pl.BlockSpec((dr, d), lambda i, j: (i * n_j + j, 0)),
        ]
        out_shape += [jax.ShapeDtypeStruct((n_j, d, tf), BF16),
                      jax.ShapeDtypeStruct((n_j, d, tf), BF16),
                      jax.ShapeDtypeStruct((f, d), BF16)]
    outs = pl.pallas_call(
        functools.partial(_ffn_kernel, final=final, cast_next=nxt is not None),
        grid=(n_i, n_j),
        in_specs=in_specs,
        out_specs=out_specs,
        out_shape=out_shape,
        scratch_shapes=[
            pltpu.VMEM((2, tm, d), F32),
            pltpu.VMEM((tm, d), BF16),
            pltpu.VMEM((tm, d), F32),
            pltpu.SemaphoreType.DMA((2,)),
            pltpu.SemaphoreType.DMA((2,)),
        ],
        compiler_params=_cparams("arbitrary", "arbitrary"),
        name="ffn",
    )(*operands)
    return outs[0], tuple(outs[1:])


def _causal_conv(buf_ref, w_ref, out_ref, *, taps, halo, rows, width):
    base = halo - (taps - 1)
    win = CONV_ROWS + halo
    for c0 in range(0, width, LANES):
        cs = slice(c0, c0 + LANES)
        wts = w_ref[:, cs]
        for r0 in range(rows.start, rows.stop, CONV_ROWS):
            window = buf_ref[r0:r0 + win, cs]
            acc = None
            for r in range(SUBLANES):
                ks = [k for k in range(taps) if (base + k) % SUBLANES == r]
                if not ks:
                    continue
                shifted = window if r == 0 else pltpu.roll(window, win - r, axis=0)
                for k in ks:
                    q = (base + k) // SUBLANES * SUBLANES
                    term = wts[k:k + 1, :] * shifted[q:q + CONV_ROWS, :]
                    acc = term if acc is None else acc + term
            out_ref[r0:r0 + CONV_ROWS, cs] = acc


def _even_kernel(x_ref, shift_ref, scale_ref, gate_ref, g_ref, wa_ref, wr_ref, aconv_ref, lng_ref,
                 lnb_ref, bconv_ref, wout_ref, o_ref, h_ref, abuf, cbuf, bb_ref, ca_ref, cb_ref, *,
                 tiles_per_seq):
    tm = x_ref.shape[0]
    ng, _, cg = bb_ref.shape
    first = (pl.program_id(0) % tiles_per_seq) == 0

    @pl.when(first)
    def _():
        abuf[:, 0:A_HALO, :] = jnp.zeros((ng, A_HALO, cg), F32)
        cbuf[:, 0:B_HALO, :] = jnp.zeros((ng, B_HALO, cg), F32)

    @pl.when(jnp.logical_not(first))
    def _():
        abuf[:, 0:A_HALO, :] = abuf[:, tm:tm + A_HALO, :]
        cbuf[:, 0:B_HALO, :] = cbuf[:, tm:tm + B_HALO, :]

    x = x_ref[...]
    h_ref[...] = _norm_modulate(x, g_ref[...], shift_ref[...], scale_ref[...]).astype(BF16)

    h = h_ref[...]
    a = (jnp.dot(h, wa_ref[0], preferred_element_type=F32)
         * jax.nn.sigmoid(jnp.dot(h, wa_ref[1], preferred_element_type=F32)))
    for c in range(ng):
        abuf[c, A_HALO:A_HALO + tm, :] = a[:, c * cg:(c + 1) * cg]

    def group(c, carry):
        _causal_conv(abuf.at[c], aconv_ref.at[c], ca_ref.at[c], taps=A_KERNEL, halo=A_HALO,
                     rows=slice(0, tm), width=cg)
        piece = jnp.dot(h_ref[...], wr_ref[c], preferred_element_type=F32)
        bb_ref[c] = piece[:, 0:cg]
        cbuf[c, B_HALO:B_HALO + tm, :] = piece[:, cg:2 * cg] * piece[:, 2 * cg:3 * cg]
        _causal_conv(cbuf.at[c], bconv_ref.at[c], cb_ref.at[c], taps=B_KERNEL, halo=B_HALO,
                     rows=slice(0, tm), width=cg)
        return carry

    lax.fori_loop(0, ng, group, 0)

    conv = jnp.concatenate([ca_ref[c] for c in range(ng)], axis=1)
    mu = jnp.mean(conv, axis=-1, keepdims=True)
    ac = conv - mu
    var = jnp.mean(ac * ac, axis=-1, keepdims=True)
    a = _silu(ac * lax.rsqrt(var + EPS) * lng_ref[...] + lnb_ref[...]).astype(BF16)
    bo = jnp.concatenate([bb_ref[c] * cb_ref[c] for c in range(ng)], axis=1).astype(BF16)
    y = (jnp.dot(a, wout_ref[0:A_WIDTH, :], preferred_element_type=F32)
         + jnp.dot(bo, wout_ref[A_WIDTH:A_WIDTH + B_WIDTH, :], preferred_element_type=F32))
    o_ref[...] = x + (1.0 + gate_ref[...]) * y


def _even_mix(x2, mod5, norm_g4, w_in, a_conv, ln_g, ln_b, b_conv, w_out, l, jx):
    rows, d = x2.shape
    tm, cg = EVEN_TM, EVEN_CG
    tps = SEQ // tm
    ng = A_WIDTH // cg
    assert A_WIDTH == B_WIDTH
    wa = _tile_major([w_in], (jx,), A_WIDTH, ntiles=2)
    rest_first = [(2 * A_WIDTH + k * B_WIDTH) // cg for k in range(3)]
    wr = _tile_major([w_in] * 3, (jx,), cg, ntiles=ng, first=rest_first)
    aconv = a_conv[jx].reshape(A_KERNEL, ng, cg).transpose(1, 0, 2)
    bconv = b_conv[jx].reshape(B_KERNEL, ng, cg).transpose(1, 0, 2)

    def whole(shape):
        return pl.BlockSpec(shape, lambda i: (0,) * len(shape))

    return pl.pallas_call(
        functools.partial(_even_kernel, tiles_per_seq=tps),
        grid=(rows // tm,),
        in_specs=[
            pl.BlockSpec((tm, d), lambda i: (i, 0)),
            _mod_spec(l, 1, 0, tps), _mod_spec(l, 1, 1, tps), _mod_spec(l, 1, 2, tps),
            pl.BlockSpec((None, None, 1, d), lambda i: (l, 1, 0, 0)),
            whole((2, d, A_WIDTH)),
            whole((ng, d, 3 * cg)),
            whole((ng, A_KERNEL, cg)),
            pl.BlockSpec((None, 1, A_WIDTH), lambda i: (jx, 0, 0)),
            pl.BlockSpec((None, 1, A_WIDTH), lambda i: (jx, 0, 0)),
            whole((ng, B_KERNEL, cg)),
            pl.BlockSpec((None, A_WIDTH + B_WIDTH, d), lambda i: (jx, 0, 0),
                         pipeline_mode=pl.Buffered(1)),
        ],
        out_specs=pl.BlockSpec((tm, d), lambda i: (i, 0)),
        out_shape=jax.ShapeDtypeStruct((rows, d), F32),
        scratch_shapes=[
            pltpu.VMEM((tm, d), BF16),
            pltpu.VMEM((ng, A_HALO + tm, cg), F32),
            pltpu.VMEM((ng, B_HALO + tm, cg), F32),
            pltpu.VMEM((ng, tm, cg), F32),
            pltpu.VMEM((ng, tm, cg), F32),
            pltpu.VMEM((ng, tm, cg), F32),
        ],
        compiler_params=_cparams("arbitrary"),
        name="even_mix",
    )(x2, mod5, mod5, mod5, norm_g4, wa, wr, aconv, ln_g, ln_b, bconv, w_out)


def _odd_kernel(x_ref, shift_ref, scale_ref, gate_ref, g_ref, win_ref, lng_ref, lnb_ref, ws_ref,
                bs_ref, wout_ref, o_ref, h_ref, v_ref, acc_ref, mu_ref, rstd_ref, *, nt):
    j = pl.program_id(1)
    tm = x_ref.shape[0]
    width = nt * win_ref.shape[1]
    edge_rows = _row_slices(tm, ODD_SPLIT)

    def in_proj(rs):
        return _gelu(jnp.dot(h_ref[rs, :], win_ref[...], preferred_element_type=F32))

    def row_stats(rs):
        total = jnp.zeros((rs.stop - rs.start, 1), F32)
        for t in range(nt):
            total = total + jnp.sum(v_ref[t, rs, :], axis=-1, keepdims=True)
        mu = total * (1.0 / width)
        sq = jnp.zeros_like(mu)
        for t in range(nt):
            vc = v_ref[t, rs, :] - mu
            sq = sq + jnp.sum(vc * vc, axis=-1, keepdims=True)
        mu_ref[rs, :] = mu
        rstd_ref[rs, :] = lax.rsqrt(sq * (1.0 / width) + EPS)

    @pl.when(j == 0)
    def _():
        for rs in edge_rows:
            h_ref[rs, :] = _norm_modulate(
                x_ref[rs, :], g_ref[...], shift_ref[...], scale_ref[...]).astype(BF16)
            v_ref[0, rs, :] = in_proj(rs)
            acc_ref[rs, :] = jnp.zeros((rs.stop - rs.start, acc_ref.shape[1]), F32)

    @pl.when(jnp.logical_and(j > 0, j < nt - 1))
    def _():
        v_ref[j] = in_proj(slice(0, tm))

    @pl.when(j == nt - 1)
    def _():
        for rs in edge_rows:
            v_ref[nt - 1, rs, :] = in_proj(rs)
            row_stats(rs)

    def gated_step(is_last):
        t = j - nt
        row = lax.broadcasted_iota(jnp.int32, (CHUNK, CHUNK), 0)
        col = lax.broadcasted_iota(jnp.int32, (CHUNK, CHUNK), 1)
        heads = ws_ref.shape[0]
        ws = [jnp.where(col <= row, ws_ref[hh], 0.0).astype(BF16) for hh in range(heads)]
        for rs in edge_rows:
            vn = ((v_ref[t, rs, :] - mu_ref[rs, :]) * rstd_ref[rs, :] * lng_ref[...]
                  + lnb_ref[...]).astype(BF16)
            u = in_proj(rs)
            gated = []
            for c0 in range(0, rs.stop - rs.start, CHUNK):
                cr = slice(c0, c0 + CHUNK)
                parts = []
                for hh in range(heads):
                    cs = slice(hh * C_HEAD_DIM, (hh + 1) * C_HEAD_DIM)
                    vs = jnp.dot(ws[hh], vn[cr, cs], preferred_element_type=F32) + bs_ref[hh]
                    parts.append((u[cr, cs] * vs).astype(BF16))
                gated.append(jnp.concatenate(parts, axis=1))
            gated = jnp.concatenate(gated, axis=0) if len(gated) > 1 else gated[0]
            y = jnp.dot(gated, wout_ref[...], preferred_element_type=F32)
            if is_last:
                o_ref[rs, :] = x_ref[rs, :] + (1.0 + gate_ref[...]) * (acc_ref[rs, :] + y)
            else:
                acc_ref[rs, :] += y

    @pl.when(jnp.logical_and(j >= nt, j < 2 * nt - 1))
    def _():
        gated_step(False)

    @pl.when(j == 2 * nt - 1)
    def _():
        gated_step(True)


def _odd_mix(x2, mod5, norm_g4, w_in, ln_g, ln_b, w_s, b_s4, w_out, l, jx):
    rows, d = x2.shape
    tm, tn = ODD_TM, ODD_TN
    tps = SEQ // tm
    nt = C_WIDTH // tn
    hpt = tn // C_HEAD_DIM

    def u_tile(j):
        return jnp.maximum(j - nt, 0)

    return pl.pallas_call(
        functools.partial(_odd_kernel, nt=nt),
        grid=(rows // tm, 2 * nt),
        in_specs=[
            pl.BlockSpec((tm, d), lambda i, j: (i, 0)),
            _mod_spec(l, 1, 0, tps), _mod_spec(l, 1, 1, tps), _mod_spec(l, 1, 2, tps),
            pl.BlockSpec((None, None, 1, d), lambda i, j: (l, 1, 0, 0)),
            pl.BlockSpec((None, d, tn), lambda i, j: (jnp.where(j < nt, j + nt, j - nt), 0, 0)),
            pl.BlockSpec((None, 1, tn), lambda i, j: (jx, 0, u_tile(j))),
            pl.BlockSpec((None, 1, tn), lambda i, j: (jx, 0, u_tile(j))),
            pl.BlockSpec((None, hpt, CHUNK, CHUNK), lambda i, j: (jx, u_tile(j), 0, 0)),
            pl.BlockSpec((None, hpt, CHUNK, 1), lambda i, j: (jx, u_tile(j), 0, 0)),
            pl.BlockSpec((None, tn, d), lambda i, j: (jx, u_tile(j), 0)),
        ],
        out_specs=pl.BlockSpec((tm, d), lambda i, j: (i, 0)),
        out_shape=jax.ShapeDtypeStruct((rows, d), F32),
        scratch_shapes=[
            pltpu.VMEM((tm, d), BF16),
            pltpu.VMEM((nt, tm, tn), F32),
            pltpu.VMEM((tm, d), F32),
            pltpu.VMEM((tm, 1), F32),
            pltpu.VMEM((tm, 1), F32),
        ],
        compiler_params=_cparams("arbitrary", "arbitrary"),
        name="odd_mix",
    )(x2, mod5, mod5, mod5, norm_g4, w_in, ln_g, ln_b, w_s, b_s4, w_out)


def kernel(x, c, ada_w, ada_b, norm_g, ffn_w_gate, ffn_w_up, ffn_w_down, ev_w_in, ev_a_conv,
           ev_a_ln_g, ev_a_ln_b, ev_b_conv, ev_w_out, od_w_in, od_v_ln_g, od_v_ln_b, od_w_s,
           od_b_s, od_w_out, final_g):
    bsz, seq, d = x.shape
    depth = ada_w.shape[0]
    assert (seq, d) == (SEQ, D_MODEL)

    mod5 = _modulation(c, ada_w, ada_b).reshape(depth, bsz, N_SUB * 3, 1, d)
    norm_g4 = norm_g.reshape(depth, N_SUB, 1, d)
    final_g2 = final_g.reshape(1, d)
    ffn_w = (ffn_w_gate, ffn_w_up, ffn_w_down)
    w_bf16 = (_tile_major([ffn_w_gate], (0, 0), FFN_TF), _tile_major([ffn_w_up], (0, 0), FFN_TF),
              ffn_w_down[0, 0].astype(BF16))
    ev_out = ev_w_out.astype(BF16)
    od_out = od_w_out.astype(BF16)
    ev_ln_g, ev_ln_b = (v.reshape(-1, 1, A_WIDTH) for v in (ev_a_ln_g, ev_a_ln_b))
    od_ln_g, od_ln_b = (v.reshape(-1, 1, C_WIDTH) for v in (od_v_ln_g, od_v_ln_b))
    od_bs = od_b_s.reshape(od_b_s.shape[0], -1, CHUNK, 1)

    x2 = x.reshape(bsz * seq, d)
    for l in range(depth):
        x2, w_bf16 = _ffn(x2, mod5, norm_g4, *w_bf16, final_g2, l, 0, False, ffn_w + (l, 1))
        jx = l // 2
        if l % 2 == 0:
            x2 = _even_mix(x2, mod5, norm_g4, ev_w_in, ev_a_conv, ev_ln_g, ev_ln_b, ev_b_conv,
                           ev_out, l, jx)
        else:
            od_in = _tile_major([od_w_in], (jx,), ODD_TN)
            x2 = _odd_mix(x2, mod5, norm_g4, od_in, od_ln_g, od_ln_b, od_w_s, od_bs, od_out,
                          l, jx)
        is_last = l == depth - 1
        x2, w_bf16 = _ffn(x2, mod5, norm_g4, *w_bf16, final_g2, l, 2, is_last,
                          None if is_last else ffn_w + (l + 1, 0))
    return x2.reshape(bsz, seq, d)
```
